```python
import math
import jax
import jax.numpy as jnp
from jax import lax
import numpy as np

D_MODEL = 1024
BATCH = 2
SEQ = 8192
DEPTH = 2

HEAD_DIM = 64
ROPE_THETA = 10000.0
NORM_EPS = 1e-6
PLE_DIM = 256
BLOCK = 128

A_HEADS = 8
IDX_HEADS = 8
IDX_DIM = 64
TOPK_MAX = 256

B_HEADS = 8
B_KV_HEADS = 2
B_WINDOW = 128

C_HEADS = 8
C_PATTERNS = ((128, 1), (512, 4), (2048, 16))

D_HEADS = 4
D_QK_DIM = 64
D_V_DIM = 128

A_WIDTH = A_HEADS * HEAD_DIM
B_WIDTH = B_HEADS * HEAD_DIM
C_WIDTH = C_HEADS * HEAD_DIM
D_WIDTH = D_HEADS * D_V_DIM

EVEN_SPLITS = (A_WIDTH, HEAD_DIM, HEAD_DIM, IDX_HEADS * IDX_DIM, IDX_DIM, IDX_HEADS, A_WIDTH,
               B_WIDTH, B_KV_HEADS * HEAD_DIM, B_KV_HEADS * HEAD_DIM, B_WIDTH)
ODD_SPLITS = (C_WIDTH, C_WIDTH, C_WIDTH, C_WIDTH,
              D_HEADS * 2 * D_QK_DIM, D_HEADS * 2 * D_QK_DIM, D_WIDTH, D_WIDTH)

kernel_name = 'hybrid_dsa_swa_dilated_diff_trunk'


def _rms_norm(x, gain):
    xf = x.astype(jnp.float32)
    y = xf * lax.rsqrt(jnp.mean(xf * xf, axis=-1, keepdims=True) + NORM_EPS)
    return (y * gain.astype(jnp.float32)).astype(x.dtype)


def _rope(x, pos):
    half = x.shape[-1] // 2
    inv = ROPE_THETA ** (-jnp.arange(half, dtype=jnp.float32) / half)
    ang = pos.astype(jnp.float32)[:, None] * inv[None, :]
    cos = jnp.cos(ang)[None, :, None, :]
    sin = jnp.sin(ang)[None, :, None, :]
    xf = x.astype(jnp.float32)
    x1, x2 = xf[..., :half], xf[..., half:]
    return jnp.concatenate([x1 * cos - x2 * sin, x2 * cos + x1 * sin], axis=-1).astype(x.dtype)


def _split(h, sizes):
    offs = np.cumsum(sizes)[:-1].tolist()
    return jnp.split(h, offs, axis=-1)


def _to_blocks(x):
    b, s = x.shape[:2]
    x = x.reshape((b, s // BLOCK, BLOCK) + x.shape[2:])
    return jnp.moveaxis(x, 1, 0)


def _from_blocks(x):
    x = jnp.moveaxis(x, 0, 1)
    return x.reshape((x.shape[0], x.shape[1] * x.shape[2]) + x.shape[3:])


def _dsa_attention(q, k, v, q_idx, k_idx, w_idx):
    bsz, seq = q.shape[:2]
    top_k = min(TOPK_MAX, seq // 4)
    key_pos = jnp.arange(seq)
    bidx = jnp.arange(bsz)[:, None, None]
    scale = HEAD_DIM ** -0.5
    idx_scale = (IDX_HEADS * IDX_DIM) ** -0.5

    def block(args):
        qb, qib, wib, start = args
        t = start + jnp.arange(BLOCK)
        causal = key_pos[None, :] <= t[:, None]
        rel = jax.nn.relu(jnp.einsum('bqhd,bsd->bqhs', qib, k_idx,
                                     preferred_element_type=jnp.float32))
        score = jnp.einsum('bqhs,bqh->bqs', rel, wib.astype(jnp.float32)) * idx_scale
        score = jnp.where(causal[None], score, -jnp.inf)
        _, sel = lax.top_k(score, top_k)
        valid = sel <= t[None, :, None]
        kg = k[bidx, sel]
        vg = v[bidx, sel]
        s = jnp.einsum('bqhd,bqkd->bqhk', qb, kg, preferred_element_type=jnp.float32) * scale
        s = jnp.where(valid[:, :, None, :], s, -jnp.inf)
        pr = jax.nn.softmax(s, axis=-1).astype(v.dtype)
        return jnp.einsum('bqhk,bqkd->bqhd', pr, vg)

    starts = jnp.arange(seq // BLOCK) * BLOCK
    out = lax.map(block, (_to_blocks(q), _to_blocks(q_idx), _to_blocks(w_idx), starts))
    return _from_blocks(out)


def _sliding_window_sink_attention(q, k, v, sinks):
    bsz, seq = q.shape[:2]
    nb = seq // BLOCK
    grp = B_HEADS // B_KV_HEADS
    scale = HEAD_DIM ** -0.5
    qb = q.reshape(bsz, nb, BLOCK, B_KV_HEADS, grp, HEAD_DIM)

    def band(x):
        xb = x.reshape(bsz, nb, BLOCK, B_KV_HEADS, HEAD_DIM)
        prev = jnp.pad(xb, ((0, 0), (1, 0), (0, 0), (0, 0), (0, 0)))[:, :-1]
        return jnp.concatenate([prev, xb], axis=2)

    kb, vb = band(k), band(v)
    s = jnp.einsum('bnqkgd,bnskd->bnkgqs', qb, kb, preferred_element_type=jnp.float32) * scale
    qi = jnp.arange(BLOCK)[:, None]
    kj = jnp.arange(2 * BLOCK)[None, :]
    dist = qi + BLOCK - kj
    in_win = (dist >= 0) & (dist < B_WINDOW)
    mask = in_win[None] & ((jnp.arange(nb)[:, None, None] > 0) | (kj[None] >= BLOCK))
    s = jnp.where(mask[None, :, None, None], s, -jnp.inf)
    sink = sinks.astype(jnp.float32).reshape(1, 1, B_KV_HEADS, grp, 1, 1)
    m = jnp.maximum(jnp.max(s, axis=-1, keepdims=True), sink)
    e = jnp.exp(s - m)
    denom = jnp.sum(e, axis=-1, keepdims=True) + jnp.exp(sink - m)
    pr = (e / denom).astype(v.dtype)
    o = jnp.einsum('bnkgqs,bnskd->bnqkgd', pr, vb)
    return o.reshape(bsz, seq, B_HEADS, HEAD_DIM)


def _dilated_attention(q, k, v):
    seq = q.shape[1]
    scale = HEAD_DIM ** -0.5

    def block(args):
        qb, start = args
        t = start + jnp.arange(BLOCK)
        lses, outs = [], []
        for window, dilation in C_PATTERNS:
            n_keys = window // dilation + 1
            idx = t[:, None] - dilation * jnp.arange(n_keys)[None, :]
            valid = idx >= 0
            idx = jnp.maximum(idx, 0)
            kg = k[:, idx]
            vg = v[:, idx]
            s = jnp.einsum('bqhd,bqnhd->bqhn', qb, kg, preferred_element_type=jnp.float32) * scale
            s = jnp.where(valid[None, :, None, :], s, -jnp.inf)
            m = jnp.max(s, axis=-1, keepdims=True)
            e = jnp.exp(s - m)
            den = jnp.sum(e, axis=-1, keepdims=True)
            o = jnp.einsum('bqhn,bqnhd->bqhd', e.astype(v.dtype), vg,
                           preferred_element_type=jnp.float32) / den
            lses.append(m[..., 0] + jnp.log(den[..., 0]))
            outs.append(o)
        wts = jax.nn.softmax(jnp.stack(lses, axis=0), axis=0)
        comb = jnp.sum(wts[..., None] * jnp.stack(outs, axis=0), axis=0)
        return comb.astype(q.dtype)

    starts = jnp.arange(seq // BLOCK) * BLOCK
    return _from_blocks(lax.map(block, (_to_blocks(q), starts)))


def _diff_attention(q, k, v, lam, sub_gain, lambda_init):
    seq = q.shape[1]
    key_pos = jnp.arange(seq)
    scale = D_QK_DIM ** -0.5

    def block(args):
        qb, start = args
        t = start + jnp.arange(BLOCK)
        s = jnp.einsum('bqhcd,bshcd->bhcqs', qb, k, preferred_element_type=jnp.float32) * scale
        causal = key_pos[None, :] <= t[:, None]
        s = jnp.where(causal, s, -jnp.inf)
        pr = jax.nn.softmax(s, axis=-1)
        a = pr[:, :, 0] - lam * pr[:, :, 1]
        return jnp.einsum('bhqs,bshd->bqhd', a.astype(v.dtype), v)

    starts = jnp.arange(seq // BLOCK) * BLOCK
    o = _from_blocks(lax.map(block, (_to_blocks(q), starts)))
    return _rms_norm(o, sub_gain) * (1.0 - lambda_init)


def _even_layer(h, pos, w_in, w_out, a_q_gain, a_k_gain, idx_k_gain, b_q_gain, b_k_gain, b_sinks):
    bsz, seq, _ = h.shape
    aq, ak, av, iq, ik, iw, ag, bq, bk, bv, bg = _split(h @ w_in, EVEN_SPLITS)
    aq = _rope(_rms_norm(aq.reshape(bsz, seq, A_HEADS, HEAD_DIM), a_q_gain), pos)
    ak = _rope(_rms_norm(ak.reshape(bsz, seq, 1, HEAD_DIM), a_k_gain), pos)[:, :, 0]
    iq = _rope(iq.reshape(bsz, seq, IDX_HEADS, IDX_DIM), pos)
    ik = _rope(_rms_norm(ik.reshape(bsz, seq, 1, IDX_DIM), idx_k_gain), pos)[:, :, 0]
    oa = _dsa_attention(aq, ak, av, iq, ik, iw)
    bq = _rope(_rms_norm(bq.reshape(bsz, seq, B_HEADS, HEAD_DIM), b_q_gain), pos)
    bk = _rope(_rms_norm(bk.reshape(bsz, seq, B_KV_HEADS, HEAD_DIM), b_k_gain), pos)
    bv = bv.reshape(bsz, seq, B_KV_HEADS, HEAD_DIM)
    ob = _sliding_window_sink_attention(bq, bk, bv, b_sinks)
    ya = oa.reshape(bsz, seq, A_WIDTH) * jax.nn.silu(ag)
    yb = ob.reshape(bsz, seq, B_WIDTH) * jax.nn.silu(bg)
    return jnp.concatenate([ya, yb], axis=-1) @ w_out


def _odd_layer(h, pos, w_in, w_out, c_q_gain, c_k_gain, d_q_gain, d_k_gain,
               lq1, lk1, lq2, lk2, sub_gain, lambda_init):
    bsz, seq, _ = h.shape
    cq, ck, cv, cg, dq, dk, dv, dg = _split(h @ w_in, ODD_SPLITS)
    cq = _rope(_rms_norm(cq.reshape(bsz, seq, C_HEADS, HEAD_DIM), c_q_gain), pos)
    ck = _rope(_rms_norm(ck.reshape(bsz, seq, C_HEADS, HEAD_DIM), c_k_gain), pos)
    cv = cv.reshape(bsz, seq, C_HEADS, HEAD_DIM)
    oc = _dilated_attention(cq, ck, cv)
    dq = _rms_norm(dq.reshape(bsz, seq, 2 * D_HEADS, D_QK_DIM), d_q_gain)
    dk = _rms_norm(dk.reshape(bsz, seq, 2 * D_HEADS, D_QK_DIM), d_k_gain)
    dq = _rope(dq, pos).reshape(bsz, seq, D_HEADS, 2, D_QK_DIM)
    dk = _rope(dk, pos).reshape(bsz, seq, D_HEADS, 2, D_QK_DIM)
    dv = dv.reshape(bsz, seq, D_HEADS, D_V_DIM)
    f32 = jnp.float32
    lam = (jnp.exp(jnp.sum(lq1.astype(f32) * lk1.astype(f32)))
           - jnp.exp(jnp.sum(lq2.astype(f32) * lk2.astype(f32))) + lambda_init)
    od = _diff_attention(dq, dk, dv, lam, sub_gain, lambda_init)
    yc = oc.reshape(bsz, seq, C_WIDTH) * jax.nn.silu(cg)
    yd = od.reshape(bsz, seq, D_WIDTH) * jax.nn.silu(dg)
    return jnp.concatenate([yc, yd], axis=-1) @ w_out


def setup_inputs(seed: int = 0) -> dict:
    key = jax.random.key(seed)
    ks = jax.random.split(key, 25)
    n_even = (DEPTH + 1) // 2
    n_odd = DEPTH // 2

    def nrm(k, shape, scale):
        return jax.random.normal(k, shape, jnp.float32) * scale

    def gain(k, shape):
        return 1.0 + 0.02 * jax.random.normal(k, shape, jnp.float32)

    even_out = A_WIDTH + B_WIDTH
    odd_out = C_WIDTH + D_WIDTH
    return {
        'x': nrm(ks[0], (BATCH, SEQ, D_MODEL), 1.0),
        'p': nrm(ks[1], (DEPTH, BATCH, SEQ, PLE_DIM), 1.0),
        'norm_gain': gain(ks[2], (DEPTH, D_MODEL)),
        'w_in_even': nrm(ks[3], (n_even, D_MODEL, sum(EVEN_SPLITS)), D_MODEL ** -0.5),
        'w_out_even': nrm(ks[4], (n_even, even_out, D_MODEL), even_out ** -0.5),
        'a_q_gain': gain(ks[5], (n_even, HEAD_DIM)),
        'a_k_gain': gain(ks[6], (n_even, HEAD_DIM)),
        'idx_k_gain': gain(ks[7], (n_even, IDX_DIM)),
        'b_q_gain': gain(ks[8], (n_even, HEAD_DIM)),
        'b_k_gain': gain(ks[9], (n_even, HEAD_DIM)),
        'b_sinks': nrm(ks[10], (n_even, B_HEADS), 0.5),
        'w_in_odd': nrm(ks[11], (n_odd, D_MODEL, sum(ODD_SPLITS)), D_MODEL ** -0.5),
        'w_out_odd': nrm(ks[12], (n_odd, odd_out, D_MODEL), odd_out ** -0.5),
        'c_q_gain': gain(ks[13], (n_odd, HEAD_DIM)),
        'c_k_gain': gain(ks[14], (n_odd, HEAD_DIM)),
        'd_q_gain': gain(ks[15], (n_odd, D_QK_DIM)),
        'd_k_gain': gain(ks[16], (n_odd, D_QK_DIM)),
        'd_lambda_q1': nrm(ks[17], (n_odd, D_QK_DIM), 0.1),
        'd_lambda_k1': nrm(ks[18], (n_odd, D_QK_DIM), 0.1),
        'd_lambda_q2': nrm(ks[19], (n_odd, D_QK_DIM), 0.1),
        'd_lambda_k2': nrm(ks[20], (n_odd, D_QK_DIM), 0.1),
        'd_subln_gain': gain(ks[21], (n_odd, D_V_DIM)),
        'ple_norm_gain': gain(ks[22], (DEPTH, D_MODEL)),
        'w_ple_gate': nrm(ks[23], (DEPTH, D_MODEL, D_MODEL), D_MODEL ** -0.5),
        'w_ple_proj': nrm(ks[24], (DEPTH, PLE_DIM, D_MODEL), PLE_DIM ** -0.5),
    }


def reference(x, p, norm_gain, w_in_even, w_out_even, a_q_gain, a_k_gain, idx_k_gain,
              b_q_gain, b_k_gain, b_sinks, w_in_odd, w_out_odd, c_q_gain, c_k_gain,
              d_q_gain, d_k_gain, d_lambda_q1, d_lambda_k1, d_lambda_q2, d_lambda_k2,
              d_subln_gain, ple_norm_gain, w_ple_gate, w_ple_proj):
    pos = jnp.arange(x.shape[1])
    for i in range(DEPTH):
        h = _rms_norm(x, norm_gain[i])
        j = i // 2
        if i % 2 == 0:
            y = _even_layer(h, pos, w_in_even[j], w_out_even[j], a_q_gain[j], a_k_gain[j],
                            idx_k_gain[j], b_q_gain[j], b_k_gain[j], b_sinks[j])
        else:
            lambda_init = 0.8 - 0.6 * math.exp(-0.3 * i)
            y = _odd_layer(h, pos, w_in_odd[j], w_out_odd[j], c_q_gain[j], c_k_gain[j],
                           d_q_gain[j], d_k_gain[j], d_lambda_q1[j], d_lambda_k1[j],
                           d_lambda_q2[j], d_lambda_k2[j], d_subln_gain[j], lambda_init)
        x = x + y
        gate = jax.nn.sigmoid(_rms_norm(x, ple_norm_gain[i]) @ w_ple_gate[i])
        x = x + (p[i] @ w_ple_proj[i]) * gate
    return x
```

```python
import functools
import math

import numpy as np
import jax
import jax.numpy as jnp
from jax import lax
from jax.experimental import pallas as pl
from jax.experimental.pallas import tpu as pltpu

F32 = jnp.float32
BF16 = jnp.bfloat16
I32 = jnp.int32

D_MODEL = 1024
HEAD_DIM = 64
ROPE_THETA = 10000.0
NORM_EPS = 1e-6
PLE_DIM = 256
TOPK_MAX = 256
B_WINDOW = 128
C_PATTERNS = ((128, 1), (512, 4), (2048, 16))
N_PAIRS = 4
LANES = 128
GROUP = 512
LOG2E = 1.4426950408889634
QSCALE = HEAD_DIM ** -0.5 * LOG2E
NEG = -1e30
INT_MIN = -(2 ** 31)
VMEM_LIMIT = 48 * 1024 * 1024


def _params(*sem):
    return pltpu.CompilerParams(dimension_semantics=sem, vmem_limit_bytes=VMEM_LIMIT)


def _norm_rope(y, gain, cos, sin_signed, bd, first_half, norm):
    if norm:
        ss = y * y
        hi = ss.astype(BF16)
        lo = (ss - hi.astype(F32)).astype(BF16)
        seg = (jnp.dot(hi, bd, preferred_element_type=F32)
               + jnp.dot(lo, bd, preferred_element_type=F32))
        y = y * lax.rsqrt(seg * (1.0 / HEAD_DIM) + NORM_EPS) * gain
    rot = jnp.where(first_half, pltpu.roll(y, 96, 1), pltpu.roll(y, 32, 1))
    return y * cos + rot * sin_signed


def _inproj_kernel(plan, x_ref, ng_ref, w_ref, cos_ref, sin_ref, gains_ref, bd_ref, *outs):
    x = x_ref[...]
    ms = jnp.mean(x * x, axis=-1, keepdims=True)
    h = (x * lax.rsqrt(ms + NORM_EPS) * ng_ref[...]).astype(BF16)
    cos = cos_ref[...]
    sin = sin_ref[...]
    bd = bd_ref[...]
    tm = x.shape[0]
    lane = lax.broadcasted_iota(I32, (tm, LANES), 1)
    lo_half = lane < HEAD_DIM
    first_half = (lane & (HEAD_DIM - 1)) < HEAD_DIM // 2
    one_at_64 = jnp.where(lane == HEAD_DIM, 1.0, 0.0)
    one_at_0 = jnp.where(lane == 0, 1.0, 0.0)
    for g in range(len(plan) // 4):
        yg = jnp.dot(h, w_ref[:, g * GROUP:(g + 1) * GROUP], preferred_element_type=F32)
        for c4 in range(4):
            c = g * 4 + c4
            y = yg[:, c4 * LANES:(c4 + 1) * LANES]
            step = plan[c]
            kind, out = step[0], outs[step[1]]
            if kind == "q":
                _, _, col, norm, mult = step
                y = _norm_rope(y, gains_ref[c:c + 1, :], cos, sin, bd, first_half, norm)
                out[:, col * LANES:(col + 1) * LANES] = (y * mult).astype(BF16)
            elif kind == "k":
                _, _, row = step
                y = _norm_rope(y, gains_ref[c:c + 1, :], cos, sin, bd, first_half, True)
                out[row] = y.T.astype(BF16)
            elif kind == "silu":
                _, _, col = step
                out[:, col * LANES:(col + 1) * LANES] = y * (1.0 / (1.0 + jnp.exp(-y)))
            elif kind == "v2":
                _, _, row = step
                out[row] = jnp.where(lo_half, y, one_at_64).astype(BF16)
                out[row + 1] = jnp.where(lo_half, one_at_0, y).astype(BF16)
            elif kind == "v1":
                _, _, row = step
                out[row, :, 0:LANES] = y.astype(BF16)
                out[row, :, LANES:2 * LANES] = one_at_0.astype(BF16)
            elif kind == "iw":
                _, _, mult = step
                out[...] = y * mult
            else:
                raise ValueError(kind)


def _inproj(x2, norm_gain, w, gains, cos_t, sin_t, plan, out_defs, tm):
    t_tokens = x2.shape[0]
    s_len = cos_t.shape[0]
    n_cols = w.shape[1]
    n_tiles = t_tokens // tm
    s_tiles = s_len // tm
    bd = jnp.asarray(np.kron(np.eye(2), np.ones((HEAD_DIM, HEAD_DIM))), BF16)
    out_shapes, out_specs = [], []
    for kind, n, dtype in out_defs:
        if kind == "tok":
            out_shapes.append(jax.ShapeDtypeStruct((t_tokens, n), dtype))
            out_specs.append(pl.BlockSpec((tm, n), lambda i: (i, 0)))
        elif kind == "kt":
            out_shapes.append(jax.ShapeDtypeStruct((n, LANES, t_tokens), dtype))
            out_specs.append(pl.BlockSpec((n, LANES, tm), lambda i: (0, 0, i)))
        elif kind == "v":
            out_shapes.append(jax.ShapeDtypeStruct((n, t_tokens, LANES), dtype))
            out_specs.append(pl.BlockSpec((n, tm, LANES), lambda i: (0, i, 0)))
        elif kind == "vw":
            out_shapes.append(jax.ShapeDtypeStruct((n, t_tokens, 2 * LANES), dtype))
            out_specs.append(pl.BlockSpec((n, tm, 2 * LANES), lambda i: (0, i, 0)))
    return pl.pallas_call(
        functools.partial(_inproj_kernel, plan),
        grid=(n_tiles,),
        in_specs=[
            pl.BlockSpec((tm, D_MODEL), lambda i: (i, 0)),
            pl.BlockSpec((1, D_MODEL), lambda i: (0, 0)),
            pl.BlockSpec((D_MODEL, n_cols), lambda i: (0, 0)),
            pl.BlockSpec((tm, LANES), lambda i: (i % s_tiles, 0)),
            pl.BlockSpec((tm, LANES), lambda i: (i % s_tiles, 0)),
            pl.BlockSpec(gains.shape, lambda i: (0, 0)),
            pl.BlockSpec((LANES, LANES), lambda i: (0, 0)),
        ],
        out_specs=out_specs,
        out_shape=out_shapes,
        compiler_params=_params("parallel"),
        name="inproj",
    )(x2, norm_gain.reshape(1, D_MODEL), w, cos_t, sin_t, gains, bd)


def _flash_block(q_m, kT, v, bias, m, acc):
    s = jnp.dot(q_m, kT, preferred_element_type=F32)
    if bias is not None:
        s = s + bias
    m_new = jnp.maximum(m, jnp.max(s, axis=1, keepdims=True))
    p = jnp.exp2(s - m_new).astype(BF16)
    alpha = jnp.exp2(m - m_new)
    acc = alpha * acc + jnp.dot(p, v, preferred_element_type=F32)
    return m_new, acc


def _split_heads(qc):
    lane = lax.broadcasted_iota(I32, qc.shape, 1)
    zero = jnp.zeros_like(qc)
    return jnp.where(lane < HEAD_DIM, qc, zero), jnp.where(lane >= HEAD_DIM, qc, zero)


def _merge_heads(acc_lo, acc_hi):
    lane = lax.broadcasted_iota(I32, acc_lo.shape, 1)
    o_lo = acc_lo * (1.0 / acc_lo[:, HEAD_DIM:HEAD_DIM + 1])
    o_hi = acc_hi * (1.0 / acc_hi[:, 0:1])
    return jnp.where(lane < HEAD_DIM, o_lo, o_hi)


def _dsa_kernel(tq, tk, topk, n_idx_bits, qi_ref, iw_ref, qa_ref, kT_ref, v_ref, o_ref,
                key_ref, bias_ref):
    i = pl.program_id(1)
    q0 = i * tq
    nkb = (q0 + tq + tk - 1) // tk
    row = q0 + lax.broadcasted_iota(I32, (tq, tk), 0)
    col_iota = lax.broadcasted_iota(I32, (tq, tk), 1)
    n_sub = tk // LANES

    w = iw_ref[...]
    q_idx = []
    for j in range(N_PAIRS):
        q_idx.extend(_split_heads(qi_ref[:, j * LANES:(j + 1) * LANES]))

    def score_block(kb, carry):
        k0 = pl.multiple_of(kb * tk, tk)
        kT = kT_ref[1, :, pl.ds(k0, tk)]
        acc = jnp.zeros((tq, tk), F32)
        for h in range(2 * N_PAIRS):
            s = jnp.dot(q_idx[h], kT, preferred_element_type=F32)
            acc = acc + w[:, h:h + 1] * jnp.maximum(s, 0.0)
        bits = lax.bitcast_convert_type(acc, I32)
        key = jnp.where(bits < 0, bits ^ jnp.int32(0x7FFFFFFF), bits)
        key_ref[:, pl.ds(k0, tk)] = jnp.where(col_iota + k0 <= row, key, jnp.int32(INT_MIN))
        return carry

    lax.fori_loop(0, nkb, score_block, 0)

    def count(pred):
        def blk(kb, cnt):
            k0 = pl.multiple_of(kb * tk, tk)
            ind = jnp.where(pred(key_ref[:, pl.ds(k0, tk)], k0), 1.0, 0.0)
            for c in range(n_sub):
                cnt = cnt + ind[:, c * LANES:(c + 1) * LANES]
            return cnt
        cnt = lax.fori_loop(0, nkb, blk, jnp.zeros((tq, LANES), F32))
        return jnp.sum(cnt, axis=1, keepdims=True)

    def thr_step(j, thr):
        cand = thr + lax.shift_left(jnp.int32(1), 31 - j)
        tot = count(lambda x, k0: x >= cand)
        return jnp.where(tot >= topk, cand, thr)

    thr = lax.fori_loop(0, 32, thr_step, jnp.full((tq, 1), INT_MIN, I32))

    need = topk - count(lambda x, k0: x > thr)

    def tie_step(j, p):
        cand = p + lax.shift_left(jnp.int32(1), n_idx_bits - 1 - j)
        tot = count(lambda x, k0: (x == thr) & (col_iota + k0 < cand))
        return jnp.where(tot < need, cand, p)

    tie_col = lax.fori_loop(0, n_idx_bits, tie_step, jnp.zeros((tq, 1), I32))
    tie_col = jnp.where(thr == INT_MIN, -1, tie_col)

    def bias_block(kb, carry):
        k0 = pl.multiple_of(kb * tk, tk)
        x = key_ref[:, pl.ds(k0, tk)]
        sel = (x > thr) | ((x == thr) & (col_iota + k0 <= tie_col))
        bias_ref[:, pl.ds(k0, tk)] = jnp.where(sel, 0.0, NEG)
        return carry

    lax.fori_loop(0, nkb, bias_block, 0)

    m0 = jnp.full((tq, 1), NEG, F32)
    acc0 = jnp.zeros((tq, LANES), F32)
    for j in range(N_PAIRS):
        accs = []
        for half, q_m in enumerate(_split_heads(qa_ref[:, j * LANES:(j + 1) * LANES])):
            def body(kb, carry, q_m=q_m, half=half):
                k0 = pl.multiple_of(kb * tk, tk)
                return _flash_block(q_m, kT_ref[0, :, pl.ds(k0, tk)], v_ref[half, pl.ds(k0, tk), :],
                                    bias_ref[:, pl.ds(k0, tk)], *carry)
            accs.append(lax.fori_loop(0, nkb, body, (m0, acc0))[1])
        o_ref[:, j * LANES:(j + 1) * LANES] = _merge_heads(*accs)


def _dsa_attention(qi, iw, qa, kT, v, bsz, s_len, tq, tk):
    t_tokens = qi.shape[0]
    nq = s_len // tq
    topk = min(TOPK_MAX, s_len // 4)
    n_idx_bits = max(1, int(math.ceil(math.log2(s_len))))
    return pl.pallas_call(
        functools.partial(_dsa_kernel, tq, tk, float(topk), n_idx_bits),
        grid=(bsz, nq),
        in_specs=[
            pl.BlockSpec((tq, GROUP), lambda b, i: (b * nq + i, 0)),
            pl.BlockSpec((tq, LANES), lambda b, i: (b * nq + i, 0)),
            pl.BlockSpec((tq, GROUP), lambda b, i: (b * nq + i, 0)),
            pl.BlockSpec((2, LANES, s_len), lambda b, i: (0, 0, b)),
            pl.BlockSpec((2, s_len, LANES), lambda b, i: (0, b, 0)),
        ],
        out_specs=pl.BlockSpec((tq, GROUP), lambda b, i: (b * nq + i, 0)),
        out_shape=jax.ShapeDtypeStruct((t_tokens, GROUP), F32),
        scratch_shapes=[pltpu.VMEM((tq, s_len), I32), pltpu.VMEM((tq, s_len), F32)],
        compiler_params=_params("parallel", "parallel"),
        name="dsa_attention",
    )(qi, iw, qa, kT, v)


def _band_bias_tables(tq, tk, reach, fn):
    n = (reach + tk - 1) // tq + 1
    d = np.arange(n)[:, None, None] * tq + np.arange(tq)[None, :, None] - np.arange(tk)[None, None, :]
    return jnp.asarray(fn(d), F32)


def _band_kernel(tq, tk, reach, use_sink, q_ref, kT_ref, v_ref, bias_ref, *rest):
    if use_sink:
        sink_ref, o_ref = rest
    else:
        (o_ref,) = rest
    j = pl.program_id(1)
    i = pl.program_id(2)
    q0 = i * tq
    kb_lo = jnp.maximum(q0 - reach, 0) // tk
    kb_hi = (q0 + tq - 1) // tk + 1
    lane = lax.broadcasted_iota(I32, (tq, LANES), 1)
    accs = []
    for half, q_m in enumerate(_split_heads(q_ref[...])):
        if use_sink:
            m0 = jnp.full((tq, 1), sink_ref[2 * j + half] * LOG2E, F32)
            acc0 = jnp.where(lane == (HEAD_DIM if half == 0 else 0), 1.0, 0.0)
        else:
            m0 = jnp.full((tq, 1), NEG, F32)
            acc0 = jnp.zeros((tq, LANES), F32)

        def body(kb, carry, q_m=q_m, half=half):
            k0 = pl.multiple_of(kb * tk, tk)
            return _flash_block(q_m, kT_ref[0, :, pl.ds(k0, tk)], v_ref[half, pl.ds(k0, tk), :],
                                bias_ref[(q0 - k0) // tq], *carry)
        accs.append(lax.fori_loop(kb_lo, kb_hi, body, (m0, acc0))[1])
    o_ref[...] = _merge_heads(*accs)


def _band_attention(q, kT, v, bias, sinks, k_index, v_index, bsz, s_len, tq, tk, reach, name):
    t_tokens = q.shape[0]
    nq = s_len // tq
    use_sink = sinks is not None
    in_specs = [
        pl.BlockSpec((tq, LANES), lambda b, j, i: (b * nq + i, j)),
        pl.BlockSpec((1, LANES, s_len), lambda b, j, i: (k_index(j), 0, b)),
        pl.BlockSpec((2, s_len, LANES), lambda b, j, i: (v_index(j), b, 0)),
        pl.BlockSpec(bias.shape, lambda b, j, i: (0, 0, 0)),
    ]
    args = [q, kT, v, bias]
    if use_sink:
        in_specs.append(pl.BlockSpec(memory_space=pltpu.SMEM))
        args.append(sinks)
    return pl.pallas_call(
        functools.partial(_band_kernel, tq, tk, reach, use_sink),
        grid=(bsz, N_PAIRS, nq),
        in_specs=in_specs,
        out_specs=pl.BlockSpec((tq, LANES), lambda b, j, i: (b * nq + i, j)),
        out_shape=jax.ShapeDtypeStruct((t_tokens, GROUP), F32),
        compiler_params=_params("parallel", "parallel", "parallel"),
        name=name,
    )(*args)


def _diff_kernel(tq, tk, lambda_init, q_ref, kT_ref, v_ref, bias_ref, lq1_ref, lk1_ref, lq2_ref,
                 lk2_ref, sg_ref, o_ref):
    i = pl.program_id(2)
    q0 = i * tq
    kb_last = (q0 + tq - 1) // tk
    k0_last = pl.multiple_of(kb_last * tk, tk)
    m0 = jnp.full((tq, 1), NEG, F32)
    acc0 = jnp.zeros((tq, 2 * LANES), F32)
    outs = []
    for q_m in _split_heads(q_ref[...]):
        def body(kb, carry, q_m=q_m):
            k0 = pl.multiple_of(kb * tk, tk)
            return _flash_block(q_m, kT_ref[0, :, pl.ds(k0, tk)], v_ref[0, pl.ds(k0, tk), :],
                                None, *carry)
        carry = lax.fori_loop(0, kb_last, body, (m0, acc0))
        _, acc = _flash_block(q_m, kT_ref[0, :, pl.ds(k0_last, tk)], v_ref[0, pl.ds(k0_last, tk), :],
                              bias_ref[(q0 - k0_last) // tq], *carry)
        outs.append(acc[:, 0:LANES] * (1.0 / acc[:, LANES:LANES + 1]))
    lam = (jnp.exp(jnp.sum(lq1_ref[...] * lk1_ref[...], axis=1, keepdims=True))
           - jnp.exp(jnp.sum(lq2_ref[...] * lk2_ref[...], axis=1, keepdims=True)) + lambda_init)
    o = outs[0] - lam * outs[1]
    ms = jnp.mean(o * o, axis=-1, keepdims=True)
    o_ref[...] = (o * lax.rsqrt(ms + NORM_EPS) * sg_ref[...]) * (1.0 - lambda_init)


def _diff_attention(q, kT, v, bias, lq1, lk1, lq2, lk2, sub_gain, lambda_init, k_base, bsz, s_len,
                    tq, tk):
    t_tokens = q.shape[0]
    nq = s_len // tq
    vec = lambda a: a.reshape(1, -1).astype(F32)
    small = lambda n: pl.BlockSpec((1, n), lambda b, j, i: (0, 0))
    return pl.pallas_call(
        functools.partial(_diff_kernel, tq, tk, lambda_init),
        grid=(bsz, N_PAIRS, nq),
        in_specs=[
            pl.BlockSpec((tq, LANES), lambda b, j, i: (b * nq + i, j)),
            pl.BlockSpec((1, LANES, s_len), lambda b, j, i: (k_base + j, 0, b)),
            pl.BlockSpec((1, s_len, 2 * LANES), lambda b, j, i: (j, b, 0)),
            pl.BlockSpec(bias.shape, lambda b, j, i: (0, 0, 0)),
            small(HEAD_DIM), small(HEAD_DIM), small(HEAD_DIM), small(HEAD_DIM), small(LANES),
        ],
        out_specs=pl.BlockSpec((tq, LANES), lambda b, j, i: (b * nq + i, j)),
        out_shape=jax.ShapeDtypeStruct((t_tokens, GROUP), F32),
        compiler_params=_params("parallel", "parallel", "parallel"),
        name="diff_attention",
    )(q, kT, v, bias, vec(lq1), vec(lk1), vec(lq2), vec(lk2), vec(sub_gain))


def _outproj_kernel(o1_ref, o2_ref, g_ref, x_ref, p_ref, wo_ref, pg_ref, wg_ref, wp_ref, out_ref):
    g = g_ref[...]
    y1 = (o1_ref[...] * g[:, 0:GROUP]).astype(BF16)
    y2 = (o2_ref[...] * g[:, GROUP:2 * GROUP]).astype(BF16)
    x1 = (x_ref[...]
          + jnp.dot(y1, wo_ref[0:GROUP, :], preferred_element_type=F32)
          + jnp.dot(y2, wo_ref[GROUP:2 * GROUP, :], preferred_element_type=F32))
    ms = jnp.mean(x1 * x1, axis=-1, keepdims=True)
    hn = (x1 * lax.rsqrt(ms + NORM_EPS) * pg_ref[...]).astype(BF16)
    z = jnp.dot(hn, wg_ref[...], preferred_element_type=F32)
    gate = 1.0 / (1.0 + jnp.exp(-z))
    pp = jnp.dot(p_ref[...].astype(BF16), wp_ref[...], preferred_element_type=F32)
    out_ref[...] = x1 + pp * gate


def _outproj(o1, o2, gate, x2, p2, w_out, ple_gain, w_gate, w_proj, tm):
    t_tokens = x2.shape[0]
    tok = lambda n: pl.BlockSpec((tm, n), lambda i: (i, 0))
    full = lambda a: pl.BlockSpec(a.shape, lambda i: (0, 0))
    pg = ple_gain.reshape(1, D_MODEL)
    return pl.pallas_call(
        _outproj_kernel,
        grid=(t_tokens // tm,),
        in_specs=[tok(GROUP), tok(GROUP), tok(2 * GROUP), tok(D_MODEL), tok(PLE_DIM),
                  full(w_out), full(pg), full(w_gate), full(w_proj)],
        out_specs=tok(D_MODEL),
        out_shape=jax.ShapeDtypeStruct((t_tokens, D_MODEL), F32),
        compiler_params=_params("parallel"),
        name="outproj_ple",
    )(o1, o2, gate, x2, p2, w_out, pg, w_gate, w_proj)


def _rope_tables(s_len):
    half = HEAD_DIM // 2
    inv = ROPE_THETA ** (-jnp.arange(half, dtype=F32) / half)
    ang = jnp.arange(s_len).astype(F32)[:, None] * inv[None, :]
    cos = jnp.tile(jnp.cos(ang), (1, 4))
    sin = jnp.sin(ang)
    sin_signed = jnp.tile(jnp.concatenate([-sin, sin], axis=1), (1, 2))
    return cos, sin_signed


def _pair_gain(g):
    return jnp.concatenate([g, g]).astype(F32)


def _even_layer(x2, p2, bsz, s_len, tables, norm_gain, w_in, w_out, a_q_gain, a_k_gain, idx_k_gain,
                b_q_gain, b_k_gain, b_sinks, ple_gain, w_gate, w_proj, tm, tq, tk):
    hd = HEAD_DIM
    offs = np.cumsum([0, 512, hd, hd, 512, hd, 8, 512, 512, 2 * hd, 2 * hd, 512])
    aq, ak, av, iq, ik, iw, ag, bq, bk, bv, bg = [w_in[:, offs[n]:offs[n + 1]] for n in range(11)]
    dup = lambda c: jnp.concatenate([c, c], axis=1)
    w = jnp.concatenate(
        [aq, iq, bq, ag, bg,
         dup(ak), dup(ik), dup(bk[:, :hd]), dup(bk[:, hd:]),
         dup(av), dup(bv[:, :hd]), dup(bv[:, hd:]),
         jnp.pad(iw, ((0, 0), (0, LANES - 8)))], axis=1).astype(BF16)
    ones = jnp.ones((LANES,), F32)
    gains = jnp.stack(
        [_pair_gain(a_q_gain)] * 4 + [ones] * 4 + [_pair_gain(b_q_gain)] * 4 + [ones] * 8
        + [_pair_gain(a_k_gain), _pair_gain(idx_k_gain), _pair_gain(b_k_gain), _pair_gain(b_k_gain)]
        + [ones] * 4)
    idx_scale = (8 * hd) ** -0.5
    plan = ([("q", 0, c, True, QSCALE) for c in range(4)]
            + [("q", 1, c, False, 1.0) for c in range(4)]
            + [("q", 2, c, True, QSCALE) for c in range(4)]
            + [("silu", 3, c) for c in range(8)]
            + [("k", 4, r) for r in range(4)]
            + [("v2", 5, 2 * r) for r in range(3)]
            + [("iw", 6, idx_scale)])
    out_defs = [("tok", 512, BF16), ("tok", 512, BF16), ("tok", 512, BF16), ("tok", 1024, F32),
                ("kt", 4, BF16), ("v", 6, BF16), ("tok", LANES, F32)]
    qa, qi, qb, gate, kT, v, iwt = _inproj(x2, norm_gain, w, gains, *tables, plan, out_defs, tm)
    oa = _dsa_attention(qi, iwt, qa, kT, v, bsz, s_len, tq, 512)
    win_bias = _band_bias_tables(tq, tk, B_WINDOW - 1,
                                 lambda d: np.where((d >= 0) & (d < B_WINDOW), 0.0, NEG))
    ob = _band_attention(qb, kT, v, win_bias, b_sinks.astype(F32),
                         lambda j: 2 + j // 2, lambda j: 1 + j // 2,
                         bsz, s_len, tq, tk, B_WINDOW - 1, "window_attention")
    return _outproj(oa, ob, gate, x2, p2, w_out.astype(BF16), ple_gain, w_gate.astype(BF16),
                    w_proj.astype(BF16), tm)


def _dilated_log2_multiplicity(d):
    count = np.zeros(d.shape)
    for window, dilation in C_PATTERNS:
        count += (d >= 0) & (d <= window) & (d % dilation == 0)
    return np.where(count > 0, np.log2(np.maximum(count, 1)), NEG)


def _odd_layer(x2, p2, bsz, s_len, tables, norm_gain, w_in, w_out, c_q_gain, c_k_gain, d_q_gain,
               d_k_gain, lq1, lk1, lq2, lk2, sub_gain, lambda_init, ple_gain, w_gate, w_proj,
               tm, tq, tk):
    ones = jnp.ones((LANES,), F32)
    gains = jnp.stack([_pair_gain(c_q_gain)] * 4 + [_pair_gain(c_k_gain)] * 4 + [ones] * 8
                      + [_pair_gain(d_q_gain)] * 4 + [_pair_gain(d_k_gain)] * 4 + [ones] * 8)
    plan = ([("q", 0, c, True, QSCALE) for c in range(4)]
            + [("k", 3, r) for r in range(4)]
            + [("v2", 4, 2 * r) for r in range(4)]
            + [("silu", 2, c) for c in range(4)]
            + [("q", 1, c, True, QSCALE) for c in range(4)]
            + [("k", 3, 4 + r) for r in range(4)]
            + [("v1", 5, r) for r in range(4)]
            + [("silu", 2, 4 + c) for c in range(4)])
    out_defs = [("tok", 512, BF16), ("tok", 512, BF16), ("tok", 1024, F32),
                ("kt", 8, BF16), ("v", 8, BF16), ("vw", 4, BF16)]
    qc, qd, gate, kT, vc, vd = _inproj(x2, norm_gain, w_in.astype(BF16), gains, *tables, plan,
                                       out_defs, tm)
    reach = max(wd for wd, _ in C_PATTERNS)
    dil_bias = _band_bias_tables(tq, tk, reach, _dilated_log2_multiplicity)
    oc = _band_attention(qc, kT, vc, dil_bias, None, lambda j: j, lambda j: j,
                         bsz, s_len, tq, tk, reach, "dilated_attention")
    causal_bias = _band_bias_tables(tq, tk, 0, lambda d: np.where(d >= 0, 0.0, NEG))
    od = _diff_attention(qd, kT, vd, causal_bias, lq1, lk1, lq2, lk2, sub_gain, lambda_init,
                         N_PAIRS, bsz, s_len, tq, tk)
    return _outproj(oc, od, gate, x2, p2, w_out.astype(BF16), ple_gain, w_gate.astype(BF16),
                    w_proj.astype(BF16), tm)


def kernel(x, p, norm_gain, w_in_even, w_out_even, a_q_gain, a_k_gain, idx_k_gain, b_q_gain, b_k_gain, b_sinks, w_in_odd, w_out_odd, c_q_gain, c_k_gain, d_q_gain, d_k_gain, d_lambda_q1, d_lambda_k1, d_lambda_q2, d_lambda_k2, d_subln_gain, ple_norm_gain, w_ple_gate, w_ple_proj):
    bsz, s_len, d_model = x.shape
    depth = p.shape[0]
    assert d_model == D_MODEL and s_len % 512 == 0
    tm, tq, tk = 256, 128, 256
    tables = _rope_tables(s_len)
    x2 = x.reshape(bsz * s_len, d_model)
    for i in range(depth):
        j = i // 2
        p2 = p[i].reshape(bsz * s_len, PLE_DIM)
        if i % 2 == 0:
            x2 = _even_layer(x2, p2, bsz, s_len, tables, norm_gain[i], w_in_even[j], w_out_even[j],
                             a_q_gain[j], a_k_gain[j], idx_k_gain[j], b_q_gain[j], b_k_gain[j],
                             b_sinks[j], ple_norm_gain[i], w_ple_gate[i], w_ple_proj[i], tm, tq, tk)
        else:
            lambda_init = 0.8 - 0.6 * math.exp(-0.3 * i)
            x2 = _odd_layer(x2, p2, bsz, s_len, tables, norm_gain[i], w_in_odd[j], w_out_odd[j],
                            c_q_gain[j], c_k_gain[j], d_q_gain[j], d_k_gain[j], d_lambda_q1[j],
                            d_lambda_k1[j], d_lambda_q2[j], d_lambda_k2[j], d_subln_gain[j],
                            lambda_init, ple_norm_gain[i], w_ple_gate[i], w_ple_proj[i], tm, tq, tk)
    return x2.reshape(bsz, s_len, d_model)
```

```python
import functools
import math

import numpy as np
import jax
import jax.numpy as jnp
from jax import lax
from jax.experimental import pallas as pl
from jax.experimental.pallas import tpu as pltpu

F32 = jnp.float32
BF16 = jnp.bfloat16
I32 = jnp.int32

D_MODEL = 1024
HEAD_DIM = 64
ROPE_THETA = 10000.0
NORM_EPS = 1e-6
PLE_DIM = 256
TOPK_MAX = 256
B_WINDOW = 128
C_PATTERNS = ((128, 1), (512, 4), (2048, 16))
N_PAIRS = 4
LANES = 128
GROUP = 512
LOG2E = 1.4426950408889634
QSCALE = HEAD_DIM ** -0.5 * LOG2E
NEG = -1e30
INT_MIN = -(2 ** 31)
VMEM_LIMIT = 48 * 1024 * 1024


def _params(*sem):
    return pltpu.CompilerParams(dimension_semantics=sem, vmem_limit_bytes=VMEM_LIMIT)


def _norm_rope(y, gain, cos, sin_signed, bd, first_half, norm):
    if norm:
        ss = y * y
        hi = ss.astype(BF16)
        lo = (ss - hi.astype(F32)).astype(BF16)
        seg = (jnp.dot(hi, bd, preferred_element_type=F32)
               + jnp.dot(lo, bd, preferred_element_type=F32))
        y = y * lax.rsqrt(seg * (1.0 / HEAD_DIM) + NORM_EPS) * gain
    rot = jnp.where(first_half, pltpu.roll(y, 96, 1), pltpu.roll(y, 32, 1))
    return y * cos + rot * sin_signed


def _inproj_kernel(plan, x_ref, ng_ref, w_ref, cos_ref, sin_ref, gains_ref, bd_ref, *outs):
    x = x_ref[...]
    ms = jnp.mean(x * x, axis=-1, keepdims=True)
    h = (x * lax.rsqrt(ms + NORM_EPS) * ng_ref[...]).astype(BF16)
    cos = cos_ref[...]
    sin = sin_ref[...]
    bd = bd_ref[...]
    tm = x.shape[0]
    lane = lax.broadcasted_iota(I32, (tm, LANES), 1)
    lo_half = lane < HEAD_DIM
    first_half = (lane & (HEAD_DIM - 1)) < HEAD_DIM // 2
    one_at_64 = jnp.where(lane == HEAD_DIM, 1.0, 0.0)
    one_at_0 = jnp.where(lane == 0, 1.0, 0.0)
    for g in range(len(plan) // 4):
        yg = jnp.dot(h, w_ref[:, g * GROUP:(g + 1) * GROUP], preferred_element_type=F32)
        for c4 in range(4):
            c = g * 4 + c4
            y = yg[:, c4 * LANES:(c4 + 1) * LANES]
            step = plan[c]
            kind, out = step[0], outs[step[1]]
            if kind == "q":
                _, _, col, norm, mult = step
                y = _norm_rope(y, gains_ref[c:c + 1, :], cos, sin, bd, first_half, norm)
                out[:, col * LANES:(col + 1) * LANES] = (y * mult).astype(BF16)
            elif kind == "k":
                _, _, row = step
                y = _norm_rope(y, gains_ref[c:c + 1, :], cos, sin, bd, first_half, True)
                out[row] = y.T.astype(BF16)
            elif kind == "silu":
                _, _, col = step
                out[:, col * LANES:(col + 1) * LANES] = y * (1.0 / (1.0 + jnp.exp(-y)))
            elif kind == "v2":
                _, _, row = step
                out[row] = jnp.where(lo_half, y, one_at_64).astype(BF16)
                out[row + 1] = jnp.where(lo_half, one_at_0, y).astype(BF16)
            elif kind == "v1":
                _, _, row = step
                out[row, :, 0:LANES] = y.astype(BF16)
                out[row, :, LANES:2 * LANES] = one_at_0.astype(BF16)
            elif kind == "iw":
                _, _, mult = step
                out[...] = y * mult
            else:
                raise ValueError(kind)


def _inproj(x2, norm_gain, w, gains, cos_t, sin_t, plan, out_defs, tm):
    t_tokens = x2.shape[0]
    s_len = cos_t.shape[0]
    n_cols = w.shape[1]
    n_tiles = t_tokens // tm
    s_tiles = s_len // tm
    bd = jnp.asarray(np.kron(np.eye(2), np.ones((HEAD_DIM, HEAD_DIM))), BF16)
    out_shapes, out_specs = [], []
    for kind, n, dtype in out_defs:
        if kind == "tok":
            out_shapes.append(jax.ShapeDtypeStruct((t_tokens, n), dtype))
            out_specs.append(pl.BlockSpec((tm, n), lambda i: (i, 0)))
        elif kind == "kt":
            out_shapes.append(jax.ShapeDtypeStruct((n, LANES, t_tokens), dtype))
            out_specs.append(pl.BlockSpec((n, LANES, tm), lambda i: (0, 0, i)))
        elif kind == "v":
            out_shapes.append(jax.ShapeDtypeStruct((n, t_tokens, LANES), dtype))
            out_specs.append(pl.BlockSpec((n, tm, LANES), lambda i: (0, i, 0)))
        elif kind == "vw":
            out_shapes.append(jax.ShapeDtypeStruct((n, t_tokens, 2 * LANES), dtype))
            out_specs.append(pl.BlockSpec((n, tm, 2 * LANES), lambda i: (0, i, 0)))
    return pl.pallas_call(
        functools.partial(_inproj_kernel, plan),
        grid=(n_tiles,),
        in_specs=[
            pl.BlockSpec((tm, D_MODEL), lambda i: (i, 0)),
            pl.BlockSpec((1, D_MODEL), lambda i: (0, 0)),
            pl.BlockSpec((D_MODEL, n_cols), lambda i: (0, 0)),
            pl.BlockSpec((tm, LANES), lambda i: (i % s_tiles, 0)),
            pl.BlockSpec((tm, LANES), lambda i: (i % s_tiles, 0)),
            pl.BlockSpec(gains.shape, lambda i: (0, 0)),
            pl.BlockSpec((LANES, LANES), lambda i: (0, 0)),
        ],
        out_specs=out_specs,
        out_shape=out_shapes,
        compiler_params=_params("parallel"),
        name="inproj",
    )(x2, norm_gain.reshape(1, D_MODEL), w, cos_t, sin_t, gains, bd)


def _flash_block(q_m, kT, v, bias, m, acc):
    return _flash_update(jnp.dot(q_m, kT, preferred_element_type=F32), v, bias, m, acc)


def _flash_update(s, v, bias, m, acc):
    if bias is not None:
        s = s + bias
    m_new = jnp.maximum(m, jnp.max(s, axis=1, keepdims=True))
    p = jnp.exp2(s - m_new).astype(BF16)
    alpha = jnp.exp2(m - m_new)
    acc = alpha * acc + jnp.dot(p, v, preferred_element_type=F32)
    return m_new, acc


def _flash_chains(qs, kT_of, v_of, bias_of, lo, hi, tk, carries):
    def body(kb, carry):
        k0 = pl.multiple_of(kb * tk, tk)
        bias = bias_of(k0)
        return tuple(_flash_block(q_m, kT_of(c, k0), v_of(c, k0), bias, *carry[c])
                     for c, q_m in enumerate(qs))
    return lax.fori_loop(lo, hi, body, tuple(carries))


def _flash_chains_pipelined(qs, kT_of, v_of, bias_of, lo, hi, tk, carries):
    def logits(kb):
        k0 = pl.multiple_of(kb * tk, tk)
        return tuple(jnp.dot(q_m, kT_of(c, k0), preferred_element_type=F32)
                     for c, q_m in enumerate(qs))

    def body(kb, carry):
        state, s_cur = carry
        s_next = logits(kb + 1)
        k0 = pl.multiple_of(kb * tk, tk)
        bias = bias_of(k0)
        state = tuple(_flash_update(s_cur[c], v_of(c, k0), bias, *state[c]) for c in range(len(qs)))
        return state, s_next
    return lax.fori_loop(lo, hi, body, (tuple(carries), logits(lo)))


def _split_heads(qc):
    lane = lax.broadcasted_iota(I32, qc.shape, 1)
    zero = jnp.zeros_like(qc)
    return jnp.where(lane < HEAD_DIM, qc, zero), jnp.where(lane >= HEAD_DIM, qc, zero)


def _merge_heads(acc_lo, acc_hi):
    lane = lax.broadcasted_iota(I32, acc_lo.shape, 1)
    o_lo = acc_lo * (1.0 / acc_lo[:, HEAD_DIM:HEAD_DIM + 1])
    o_hi = acc_hi * (1.0 / acc_hi[:, 0:1])
    return jnp.where(lane < HEAD_DIM, o_lo, o_hi)


def _dsa_kernel(tq, tk, topk, s_len, qi_ref, iw_ref, qa_ref, kT_ref, v_ref, o_ref, sc_ref, tie_ref):
    i = pl.program_id(1)
    q0 = i * tq
    nkb = (q0 + tq + tk - 1) // tk
    n_sub = tk // LANES
    n_heads = 2 * N_PAIRS
    row = q0 + lax.broadcasted_iota(I32, (tq, tk), 0)
    col_iota = lax.broadcasted_iota(I32, (tq, tk), 1)

    w = iw_ref[...]
    wb = [jnp.broadcast_to(w[:, h:h + 1], (tq, LANES)) for h in range(n_heads)]
    q_idx = []
    for j in range(N_PAIRS):
        q_idx.extend(_split_heads(qi_ref[:, j * LANES:(j + 1) * LANES]))

    def score_block(kb, carry):
        k0 = pl.multiple_of(kb * tk, tk)
        kT = kT_ref[1, :, pl.ds(k0, tk)]
        accs = [jnp.zeros((tq, LANES), F32)] * n_sub
        for h in range(n_heads):
            s = jnp.dot(q_idx[h], kT, preferred_element_type=F32)
            accs = [accs[c] + wb[h] * jnp.maximum(s[:, c * LANES:(c + 1) * LANES], 0.0)
                    for c in range(n_sub)]
        score = jnp.concatenate(accs, axis=1)
        sc_ref[:, pl.ds(k0, tk)] = jnp.where(col_iota + k0 <= row, score, -jnp.inf)
        return carry

    lax.fori_loop(0, nkb, score_block, 0)

    def count(pred):
        def blk(kb, cnt):
            k0 = pl.multiple_of(kb * tk, tk)
            ind = jnp.where(pred(sc_ref[:, pl.ds(k0, tk)], k0), 1.0, 0.0)
            for c in range(n_sub):
                cnt = cnt + ind[:, c * LANES:(c + 1) * LANES]
            return cnt
        cnt = lax.fori_loop(0, nkb, blk, jnp.zeros((tq, LANES), F32))
        return jnp.sum(cnt, axis=1, keepdims=True)

    def decode(code):
        bits = jnp.where(code < 0, code ^ jnp.int32(0x7FFFFFFF), code)
        return lax.bitcast_convert_type(bits, F32)

    def thr_step(j, code):
        cand = code + lax.shift_left(jnp.int32(1), 31 - j)
        cand_f = decode(cand)
        tot = count(lambda x, k0: x >= cand_f)
        return jnp.where(tot >= topk, cand, code)

    thr = decode(lax.fori_loop(0, 32, thr_step, jnp.full((tq, 1), INT_MIN, I32)))
    keep_all = (row[:, 0:1] + 1).astype(F32) <= topk

    need = topk - count(lambda x, k0: x > thr)
    n_tie = count(lambda x, k0: x == thr)
    ambiguous = jnp.where((n_tie > need) & jnp.logical_not(keep_all), 1.0, 0.0)
    tie_ref[...] = jnp.full((tq, 1), s_len, I32)

    @pl.when(jnp.max(ambiguous) > 0.0)
    def _():
        n_bits = max(1, int(math.ceil(math.log2(s_len))))

        def tie_step(j, p):
            cand = p + lax.shift_left(jnp.int32(1), n_bits - 1 - j)
            tot = count(lambda x, k0: (x == thr) & (col_iota + k0 < cand))
            return jnp.where(tot < need, cand, p)

        tie_ref[...] = lax.fori_loop(0, n_bits, tie_step, jnp.zeros((tq, 1), I32))

    tie_col = tie_ref[...]

    def bias_block(kb, carry):
        k0 = pl.multiple_of(kb * tk, tk)
        x = sc_ref[:, pl.ds(k0, tk)]
        col = col_iota + k0
        sel = (col <= row) & (keep_all | (x > thr) | ((x == thr) & (col <= tie_col)))
        sc_ref[:, pl.ds(k0, tk)] = jnp.where(sel, 0.0, NEG)
        return carry

    lax.fori_loop(0, nkb, bias_block, 0)

    q_att = []
    for j in range(N_PAIRS):
        q_att.extend(_split_heads(qa_ref[:, j * LANES:(j + 1) * LANES]))
    m0 = jnp.full((tq, 1), NEG, F32)
    acc0 = jnp.zeros((tq, LANES), F32)
    res = _flash_chains(
        q_att,
        lambda c, k0: kT_ref[0, :, pl.ds(k0, tk)],
        lambda c, k0: v_ref[c % 2, pl.ds(k0, tk), :],
        lambda k0: sc_ref[:, pl.ds(k0, tk)],
        0, nkb, tk, [(m0, acc0)] * n_heads)
    for j in range(N_PAIRS):
        o_ref[:, j * LANES:(j + 1) * LANES] = _merge_heads(res[2 * j][1], res[2 * j + 1][1])


def _dsa_attention(qi, iw, qa, kT, v, bsz, s_len, tq, tk):
    t_tokens = qi.shape[0]
    nq = s_len // tq
    topk = min(TOPK_MAX, s_len // 4)
    return pl.pallas_call(
        functools.partial(_dsa_kernel, tq, tk, float(topk), s_len),
        grid=(bsz, nq),
        in_specs=[
            pl.BlockSpec((tq, GROUP), lambda b, i: (b * nq + i, 0)),
            pl.BlockSpec((tq, LANES), lambda b, i: (b * nq + i, 0)),
            pl.BlockSpec((tq, GROUP), lambda b, i: (b * nq + i, 0)),
            pl.BlockSpec((2, LANES, s_len), lambda b, i: (0, 0, b)),
            pl.BlockSpec((2, s_len, LANES), lambda b, i: (0, b, 0)),
        ],
        out_specs=pl.BlockSpec((tq, GROUP), lambda b, i: (b * nq + i, 0)),
        out_shape=jax.ShapeDtypeStruct((t_tokens, GROUP), F32),
        scratch_shapes=[pltpu.VMEM((tq, s_len), F32), pltpu.VMEM((tq, 1), I32)],
        compiler_params=_params("parallel", "parallel"),
        name="dsa_attention",
    )(qi, iw, qa, kT, v)


def _bias_tables(n, step, tq, tk, fn):
    d = np.arange(n)[:, None, None] * step + np.arange(tq)[None, :, None] - np.arange(tk)[None, None, :]
    return jnp.asarray(fn(d), F32)


def _window_kernel(tq, q_ref, kT_ref, v_ref, bias_ref, sink_ref, o_ref):
    j = pl.program_id(1)
    i = pl.program_id(2)
    q0 = i * tq
    wk = tq + B_WINDOW
    k0 = pl.multiple_of(jnp.maximum(q0 - B_WINDOW, 0), LANES)
    kT = kT_ref[0, :, pl.ds(k0, wk)]
    bias = bias_ref[(q0 - k0) // B_WINDOW]
    lane = lax.broadcasted_iota(I32, (tq, LANES), 1)
    accs = []
    for half, q_m in enumerate(_split_heads(q_ref[...])):
        sink = sink_ref[2 * j + half] * LOG2E
        s = jnp.dot(q_m, kT, preferred_element_type=F32) + bias
        m = jnp.maximum(jnp.max(s, axis=1, keepdims=True), sink)
        p = jnp.exp2(s - m).astype(BF16)
        acc = jnp.dot(p, v_ref[half, pl.ds(k0, wk), :], preferred_element_type=F32)
        accs.append(acc + jnp.where(lane == (HEAD_DIM if half == 0 else 0), jnp.exp2(sink - m), 0.0))
    o_ref[...] = _merge_heads(*accs)


def _window_attention(q, kT, v, sinks, bsz, s_len, tq):
    t_tokens = q.shape[0]
    nq = s_len // tq
    bias = _bias_tables(2, B_WINDOW, tq, tq + B_WINDOW,
                        lambda d: np.where((d >= 0) & (d < B_WINDOW), 0.0, NEG))
    return pl.pallas_call(
        functools.partial(_window_kernel, tq),
        grid=(bsz, N_PAIRS, nq),
        in_specs=[
            pl.BlockSpec((tq, LANES), lambda b, j, i: (b * nq + i, j)),
            pl.BlockSpec((1, LANES, s_len), lambda b, j, i: (2 + j // 2, 0, b)),
            pl.BlockSpec((2, s_len, LANES), lambda b, j, i: (1 + j // 2, b, 0)),
            pl.BlockSpec(bias.shape, lambda b, j, i: (0, 0, 0)),
            pl.BlockSpec(memory_space=pltpu.SMEM),
        ],
        out_specs=pl.BlockSpec((tq, LANES), lambda b, j, i: (b * nq + i, j)),
        out_shape=jax.ShapeDtypeStruct((t_tokens, GROUP), F32),
        compiler_params=_params("parallel", "parallel", "parallel"),
        name="window_attention",
    )(q, kT, v, bias, sinks)


def _dilated_log2_multiplicity(d):
    count = np.zeros(d.shape)
    for window, dilation in C_PATTERNS:
        count += (d >= 0) & (d <= window) & (d % dilation == 0)
    return np.where(count > 0, np.log2(np.maximum(count, 1)), NEG)


def _dilated_kernel(t, reach, q_ref, kT_ref, v_ref, bias_ref, o_ref):
    i = pl.program_id(2)
    q0 = i * t
    kb_lo = jnp.maximum(q0 - reach, 0) // t
    m0 = jnp.full((t, 1), NEG, F32)
    acc0 = jnp.zeros((t, LANES), F32)
    res = _flash_chains(
        _split_heads(q_ref[...]),
        lambda c, k0: kT_ref[0, :, pl.ds(k0, t)],
        lambda c, k0: v_ref[c, pl.ds(k0, t), :],
        lambda k0: bias_ref[(q0 - k0) // t],
        kb_lo, i + 1, t, [(m0, acc0)] * 2)
    o_ref[...] = _merge_heads(res[0][1], res[1][1])


def _dilated_attention(q, kT, v, bsz, s_len, t):
    t_tokens = q.shape[0]
    nq = s_len // t
    reach = max(wd for wd, _ in C_PATTERNS)
    bias = _bias_tables((reach + t - 1) // t + 1, t, t, t, _dilated_log2_multiplicity)
    return pl.pallas_call(
        functools.partial(_dilated_kernel, t, reach),
        grid=(bsz, N_PAIRS, nq),
        in_specs=[
            pl.BlockSpec((t, LANES), lambda b, j, i: (b * nq + i, j)),
            pl.BlockSpec((1, LANES, s_len), lambda b, j, i: (j, 0, b)),
            pl.BlockSpec((2, s_len, LANES), lambda b, j, i: (j, b, 0)),
            pl.BlockSpec(bias.shape, lambda b, j, i: (0, 0, 0)),
        ],
        out_specs=pl.BlockSpec((t, LANES), lambda b, j, i: (b * nq + i, j)),
        out_shape=jax.ShapeDtypeStruct((t_tokens, GROUP), F32),
        compiler_params=_params("parallel", "parallel", "parallel"),
        name="dilated_attention",
    )(q, kT, v, bias)


def _diff_kernel(tq, tk, lambda_init, q_ref, kT_ref, v_ref, bias_ref, lq1_ref, lk1_ref, lq2_ref,
                 lk2_ref, sg_ref, o_ref):
    i = pl.program_id(2)
    n_diag = tq // tk
    qs = _split_heads(q_ref[...])
    kT_of = lambda c, k0: kT_ref[0, :, pl.ds(k0, tk)]
    v_of = lambda c, k0: v_ref[0, pl.ds(k0, tk), :]
    m0 = jnp.full((tq, 1), NEG, F32)
    acc0 = jnp.zeros((tq, 2 * LANES), F32)
    carry = _flash_chains(qs, kT_of, v_of, lambda k0: None, 0, i * n_diag, tk, [(m0, acc0)] * 2)
    for d in range(n_diag):
        k0 = pl.multiple_of(i * tq + d * tk, tk)
        carry = tuple(_flash_block(qs[c], kT_of(c, k0), v_of(c, k0), bias_ref[d], *carry[c])
                      for c in range(2))
    outs = [acc[:, 0:LANES] * (1.0 / acc[:, LANES:LANES + 1]) for _, acc in carry]
    lam = (jnp.exp(jnp.sum(lq1_ref[...] * lk1_ref[...], axis=1, keepdims=True))
           - jnp.exp(jnp.sum(lq2_ref[...] * lk2_ref[...], axis=1, keepdims=True)) + lambda_init)
    o = outs[0] - lam * outs[1]
    ms = jnp.mean(o * o, axis=-1, keepdims=True)
    o_ref[...] = (o * lax.rsqrt(ms + NORM_EPS) * sg_ref[...]) * (1.0 - lambda_init)


def _diff_attention(q, kT, v, lq1, lk1, lq2, lk2, sub_gain, lambda_init, k_base, bsz, s_len, tq, tk):
    t_tokens = q.shape[0]
    nq = s_len // tq
    bias = _bias_tables(tq // tk, -tk, tq, tk, lambda d: np.where(d >= 0, 0.0, NEG))
    vec = lambda a: a.reshape(1, -1).astype(F32)
    small = lambda n: pl.BlockSpec((1, n), lambda b, j, i: (0, 0))
    return pl.pallas_call(
        functools.partial(_diff_kernel, tq, tk, lambda_init),
        grid=(bsz, N_PAIRS, nq),
        in_specs=[
            pl.BlockSpec((tq, LANES), lambda b, j, i: (b * nq + i, j)),
            pl.BlockSpec((1, LANES, s_len), lambda b, j, i: (k_base + j, 0, b)),
            pl.BlockSpec((1, s_len, 2 * LANES), lambda b, j, i: (j, b, 0)),
            pl.BlockSpec(bias.shape, lambda b, j, i: (0, 0, 0)),
            small(HEAD_DIM), small(HEAD_DIM), small(HEAD_DIM), small(HEAD_DIM), small(LANES),
        ],
        out_specs=pl.BlockSpec((tq, LANES), lambda b, j, i: (b * nq + i, j)),
        out_shape=jax.ShapeDtypeStruct((t_tokens, GROUP), F32),
        compiler_params=_params("parallel", "parallel", "parallel"),
        name="diff_attention",
    )(q, kT, v, bias, vec(lq1), vec(lk1), vec(lq2), vec(lk2), vec(sub_gain))


def _outproj_kernel(o1_ref, o2_ref, g_ref, x_ref, p_ref, wo_ref, pg_ref, wg_ref, wp_ref, out_ref):
    g = g_ref[...]
    y1 = (o1_ref[...] * g[:, 0:GROUP]).astype(BF16)
    y2 = (o2_ref[...] * g[:, GROUP:2 * GROUP]).astype(BF16)
    x1 = (x_ref[...]
          + jnp.dot(y1, wo_ref[0:GROUP, :], preferred_element_type=F32)
          + jnp.dot(y2, wo_ref[GROUP:2 * GROUP, :], preferred_element_type=F32))
    ms = jnp.mean(x1 * x1, axis=-1, keepdims=True)
    hn = (x1 * lax.rsqrt(ms + NORM_EPS) * pg_ref[...]).astype(BF16)
    z = jnp.dot(hn, wg_ref[...], preferred_element_type=F32)
    gate = 1.0 / (1.0 + jnp.exp(-z))
    pp = jnp.dot(p_ref[...].astype(BF16), wp_ref[...], preferred_element_type=F32)
    out_ref[...] = x1 + pp * gate


def _outproj(o1, o2, gate, x2, p2, w_out, ple_gain, w_gate, w_proj, tm):
    t_tokens = x2.shape[0]
    tok = lambda n: pl.BlockSpec((tm, n), lambda i: (i, 0))
    full = lambda a: pl.BlockSpec(a.shape, lambda i: (0, 0))
    pg = ple_gain.reshape(1, D_MODEL)
    return pl.pallas_call(
        _outproj_kernel,
        grid=(t_tokens // tm,),
        in_specs=[tok(GROUP), tok(GROUP), tok(2 * GROUP), tok(D_MODEL), tok(PLE_DIM),
                  full(w_out), full(pg), full(w_gate), full(w_proj)],
        out_specs=tok(D_MODEL),
        out_shape=jax.ShapeDtypeStruct((t_tokens, D_MODEL), F32),
        compiler_params=_params("parallel"),
        name="outproj_ple",
    )(o1, o2, gate, x2, p2, w_out, pg, w_gate, w_proj)


TM = 256
TQ_DSA = 256
TK_DSA = 512
TQ_WINDOW = 256
T_DILATED = 512
TQ_DIFF = 512
TK_DIFF = 512


def _rope_tables(s_len):
    half = HEAD_DIM // 2
    inv = ROPE_THETA ** (-jnp.arange(half, dtype=F32) / half)
    ang = jnp.arange(s_len).astype(F32)[:, None] * inv[None, :]
    cos = jnp.tile(jnp.cos(ang), (1, 4))
    sin = jnp.sin(ang)
    sin_signed = jnp.tile(jnp.concatenate([-sin, sin], axis=1), (1, 2))
    return cos, sin_signed


def _pair_gain(g):
    return jnp.concatenate([g, g]).astype(F32)


def _even_layer(x2, p2, bsz, s_len, tables, norm_gain, w_in, w_out, a_q_gain, a_k_gain, idx_k_gain,
                b_q_gain, b_k_gain, b_sinks, ple_gain, w_gate, w_proj):
    hd = HEAD_DIM
    offs = np.cumsum([0, 512, hd, hd, 512, hd, 8, 512, 512, 2 * hd, 2 * hd, 512])
    aq, ak, av, iq, ik, iw, ag, bq, bk, bv, bg = [w_in[:, offs[n]:offs[n + 1]] for n in range(11)]
    dup = lambda c: jnp.concatenate([c, c], axis=1)
    w = jnp.concatenate(
        [aq, iq, bq, ag, bg,
         dup(ak), dup(ik), dup(bk[:, :hd]), dup(bk[:, hd:]),
         dup(av), dup(bv[:, :hd]), dup(bv[:, hd:]),
         jnp.pad(iw, ((0, 0), (0, LANES - 8)))], axis=1).astype(BF16)
    ones = jnp.ones((LANES,), F32)
    gains = jnp.stack(
        [_pair_gain(a_q_gain)] * 4 + [ones] * 4 + [_pair_gain(b_q_gain)] * 4 + [ones] * 8
        + [_pair_gain(a_k_gain), _pair_gain(idx_k_gain), _pair_gain(b_k_gain), _pair_gain(b_k_gain)]
        + [ones] * 4)
    idx_scale = (8 * hd) ** -0.5
    plan = ([("q", 0, c, True, QSCALE) for c in range(4)]
            + [("q", 1, c, False, 1.0) for c in range(4)]
            + [("q", 2, c, True, QSCALE) for c in range(4)]
            + [("silu", 3, c) for c in range(8)]
            + [("k", 4, r) for r in range(4)]
            + [("v2", 5, 2 * r) for r in range(3)]
            + [("iw", 6, idx_scale)])
    out_defs = [("tok", 512, BF16), ("tok", 512, BF16), ("tok", 512, BF16), ("tok", 1024, F32),
                ("kt", 4, BF16), ("v", 6, BF16), ("tok", LANES, F32)]
    qa, qi, qb, gate, kT, v, iwt = _inproj(x2, norm_gain, w, gains, *tables, plan, out_defs, TM)
    oa = _dsa_attention(qi, iwt, qa, kT, v, bsz, s_len, min(TQ_DSA, s_len), TK_DSA)
    ob = _window_attention(qb, kT, v, b_sinks.astype(F32), bsz, s_len, TQ_WINDOW)
    return _outproj(oa, ob, gate, x2, p2, w_out.astype(BF16), ple_gain, w_gate.astype(BF16),
                    w_proj.astype(BF16), TM)


def _odd_layer(x2, p2, bsz, s_len, tables, norm_gain, w_in, w_out, c_q_gain, c_k_gain, d_q_gain,
               d_k_gain, lq1, lk1, lq2, lk2, sub_gain, lambda_init, ple_gain, w_gate, w_proj):
    ones = jnp.ones((LANES,), F32)
    gains = jnp.stack([_pair_gain(c_q_gain)] * 4 + [_pair_gain(c_k_gain)] * 4 + [ones] * 8
                      + [_pair_gain(d_q_gain)] * 4 + [_pair_gain(d_k_gain)] * 4 + [ones] * 8)
    plan = ([("q", 0, c, True, QSCALE) for c in range(4)]
            + [("k", 3, r) for r in range(4)]
            + [("v2", 4, 2 * r) for r in range(4)]
            + [("silu", 2, c) for c in range(4)]
            + [("q", 1, c, True, QSCALE) for c in range(4)]
            + [("k", 3, 4 + r) for r in range(4)]
            + [("v1", 5, r) for r in range(4)]
            + [("silu", 2, 4 + c) for c in range(4)])
    out_defs = [("tok", 512, BF16), ("tok", 512, BF16), ("tok", 1024, F32),
                ("kt", 8, BF16), ("v", 8, BF16), ("vw", 4, BF16)]
    qc, qd, gate, kT, vc, vd = _inproj(x2, norm_gain, w_in.astype(BF16), gains, *tables, plan,
                                       out_defs, TM)
    oc = _dilated_attention(qc, kT, vc, bsz, s_len, T_DILATED)
    od = _diff_attention(qd, kT, vd, lq1, lk1, lq2, lk2, sub_gain, lambda_init, N_PAIRS, bsz, s_len,
                         TQ_DIFF, TK_DIFF)
    return _outproj(oc, od, gate, x2, p2, w_out.astype(BF16), ple_gain, w_gate.astype(BF16),
                    w_proj.astype(BF16), TM)


def kernel(x, p, norm_gain, w_in_even, w_out_even, a_q_gain, a_k_gain, idx_k_gain, b_q_gain, b_k_gain, b_sinks, w_in_odd, w_out_odd, c_q_gain, c_k_gain, d_q_gain, d_k_gain, d_lambda_q1, d_lambda_k1, d_lambda_q2, d_lambda_k2, d_subln_gain, ple_norm_gain, w_ple_gate, w_ple_proj):
    bsz, s_len, d_model = x.shape
    depth = p.shape[0]
    assert d_model == D_MODEL and s_len % 512 == 0
    tables = _rope_tables(s_len)
    x2 = x.reshape(bsz * s_len, d_model)
    for i in range(depth):
        j = i // 2
        p2 = p[i].reshape(bsz * s_len, PLE_DIM)
        if i % 2 == 0:
            x2 = _even_layer(x2, p2, bsz, s_len, tables, norm_gain[i], w_in_even[j], w_out_even[j],
                             a_q_gain[j], a_k_gain[j], idx_k_gain[j], b_q_gain[j], b_k_gain[j],
                             b_sinks[j], ple_norm_gain[i], w_ple_gate[i], w_ple_proj[i])
        else:
            lambda_init = 0.8 - 0.6 * math.exp(-0.3 * i)
            x2 = _odd_layer(x2, p2, bsz, s_len, tables, norm_gain[i], w_in_odd[j], w_out_odd[j],
                            c_q_gain[j], c_k_gain[j], d_q_gain[j], d_k_gain[j], d_lambda_q1[j],
                            d_lambda_k1[j], d_lambda_q2[j], d_lambda_k2[j], d_subln_gain[j],
                            lambda_init, ple_norm_gain[i], w_ple_gate[i], w_ple_proj[i])
    return x2.reshape(bsz, s_len, d_model)
```

```python
import functools
import math

import numpy as np
import jax
import jax.numpy as jnp
from jax import lax
from jax.experimental import pallas as pl
from jax.experimental.pallas import tpu as pltpu

F32 = jnp.float32
BF16 = jnp.bfloat16
I32 = jnp.int32

D_MODEL = 1024
HEAD_DIM = 64
ROPE_THETA = 10000.0
NORM_EPS = 1e-6
PLE_DIM = 256
TOPK_MAX = 256
B_WINDOW = 128
C_PATTERNS = ((128, 1), (512, 4), (2048, 16))
N_PAIRS = 4
LANES = 128
GROUP = 512
LOG2E = 1.4426950408889634
QSCALE = HEAD_DIM ** -0.5 * LOG2E
NEG = -1e30
INT_MIN = -(2 ** 31)
F32_MAX = float(np.finfo(np.float32).max)
FIXED_PROBES = 16
MAX_PROBES = 40
VMEM_LIMIT = 48 * 1024 * 1024


def _params(*sem):
    return pltpu.CompilerParams(dimension_semantics=sem, vmem_limit_bytes=VMEM_LIMIT)


def _norm_rope(y, gain, cos, sin_signed, bd, first_half, norm):
    if norm:
        ss = y * y
        hi = ss.astype(BF16)
        lo = (ss - hi.astype(F32)).astype(BF16)
        seg = (jnp.dot(hi, bd, preferred_element_type=F32)
               + jnp.dot(lo, bd, preferred_element_type=F32))
        y = y * lax.rsqrt(seg * (1.0 / HEAD_DIM) + NORM_EPS) * gain
    rot = jnp.where(first_half, pltpu.roll(y, 96, 1), pltpu.roll(y, 32, 1))
    return y * cos + rot * sin_signed


def _inproj_kernel(plan, x_ref, ng_ref, w_ref, cos_ref, sin_ref, gains_ref, bd_ref, *outs):
    x = x_ref[...]
    ms = jnp.mean(x * x, axis=-1, keepdims=True)
    h = (x * lax.rsqrt(ms + NORM_EPS) * ng_ref[...]).astype(BF16)
    cos = cos_ref[...]
    sin = sin_ref[...]
    bd = bd_ref[...]
    tm = x.shape[0]
    lane = lax.broadcasted_iota(I32, (tm, LANES), 1)
    lo_half = lane < HEAD_DIM
    first_half = (lane & (HEAD_DIM - 1)) < HEAD_DIM // 2
    one_at_64 = jnp.where(lane == HEAD_DIM, 1.0, 0.0)
    one_at_0 = jnp.where(lane == 0, 1.0, 0.0)
    for g in range(len(plan) // 4):
        yg = jnp.dot(h, w_ref[:, g * GROUP:(g + 1) * GROUP], preferred_element_type=F32)
        for c4 in range(4):
            c = g * 4 + c4
            y = yg[:, c4 * LANES:(c4 + 1) * LANES]
            step = plan[c]
            kind, out = step[0], outs[step[1]]
            if kind == "q":
                _, _, col, norm, mult = step
                y = _norm_rope(y, gains_ref[c:c + 1, :], cos, sin, bd, first_half, norm)
                out[:, col * LANES:(col + 1) * LANES] = (y * mult).astype(BF16)
            elif kind == "k":
                _, _, row = step
                y = _norm_rope(y, gains_ref[c:c + 1, :], cos, sin, bd, first_half, True)
                out[row] = y.T.astype(BF16)
            elif kind == "silu":
                _, _, col = step
                out[:, col * LANES:(col + 1) * LANES] = y * (1.0 / (1.0 + jnp.exp(-y)))
            elif kind == "v2":
                _, _, row = step
                out[row] = jnp.where(lo_half, y, one_at_64).astype(BF16)
                out[row + 1] = jnp.where(lo_half, one_at_0, y).astype(BF16)
            elif kind == "v1":
                _, _, row = step
                out[row, :, 0:LANES] = y.astype(BF16)
                out[row, :, LANES:2 * LANES] = one_at_0.astype(BF16)
            elif kind == "iw":
                _, _, mult = step
                out[...] = y * mult
            else:
                raise ValueError(kind)


def _inproj(x2, norm_gain, w, gains, cos_t, sin_t, plan, out_defs, tm):
    t_tokens = x2.shape[0]
    s_len = cos_t.shape[0]
    n_cols = w.shape[1]
    n_tiles = t_tokens // tm
    s_tiles = s_len // tm
    bd = jnp.asarray(np.kron(np.eye(2), np.ones((HEAD_DIM, HEAD_DIM))), BF16)
    out_shapes, out_specs = [], []
    for kind, n, dtype in out_defs:
        if kind == "tok":
            out_shapes.append(jax.ShapeDtypeStruct((t_tokens, n), dtype))
            out_specs.append(pl.BlockSpec((tm, n), lambda i: (i, 0)))
        elif kind == "kt":
            out_shapes.append(jax.ShapeDtypeStruct((n, LANES, t_tokens), dtype))
            out_specs.append(pl.BlockSpec((n, LANES, tm), lambda i: (0, 0, i)))
        elif kind == "v":
            out_shapes.append(jax.ShapeDtypeStruct((n, t_tokens, LANES), dtype))
            out_specs.append(pl.BlockSpec((n, tm, LANES), lambda i: (0, i, 0)))
        elif kind == "vw":
            out_shapes.append(jax.ShapeDtypeStruct((n, t_tokens, 2 * LANES), dtype))
            out_specs.append(pl.BlockSpec((n, tm, 2 * LANES), lambda i: (0, i, 0)))
    return pl.pallas_call(
        functools.partial(_inproj_kernel, plan),
        grid=(n_tiles,),
        in_specs=[
            pl.BlockSpec((tm, D_MODEL), lambda i: (i, 0)),
            pl.BlockSpec((1, D_MODEL), lambda i: (0, 0)),
            pl.BlockSpec((D_MODEL, n_cols), lambda i: (0, 0)),
            pl.BlockSpec((tm, LANES), lambda i: (i % s_tiles, 0)),
            pl.BlockSpec((tm, LANES), lambda i: (i % s_tiles, 0)),
            pl.BlockSpec(gains.shape, lambda i: (0, 0)),
            pl.BlockSpec((LANES, LANES), lambda i: (0, 0)),
        ],
        out_specs=out_specs,
        out_shape=out_shapes,
        compiler_params=_params("parallel"),
        name="inproj",
    )(x2, norm_gain.reshape(1, D_MODEL), w, cos_t, sin_t, gains, bd)


def _softmax_block(s, bias, m):
    if bias is not None:
        tq, tk = bias.shape
        s = (s.reshape(s.shape[0] // tq, tq, tk) + bias[None]).reshape(s.shape)
    m_new = jnp.maximum(m, jnp.max(s, axis=1, keepdims=True))
    return m_new, jnp.exp2(m - m_new), jnp.exp2(s - m_new).astype(BF16)


def _flash_block(q_s, kT, v, bias, m, acc):
    s = jnp.dot(q_s, kT, preferred_element_type=F32)
    m_new, alpha, p = _softmax_block(s, bias, m)
    return m_new, alpha * acc + jnp.dot(p, v, preferred_element_type=F32)


def _flash_loop(q_s, kT_of, v_of, bias_of, lo, hi, tk, m, acc):
    def body(kb, carry):
        k0 = pl.multiple_of(kb * tk, tk)
        return _flash_block(q_s, kT_of(k0), v_of(k0), bias_of(k0), *carry)
    return lax.fori_loop(lo, hi, body, (m, acc))


def _split_heads(qc):
    lane = lax.broadcasted_iota(I32, qc.shape, 1)
    zero = jnp.zeros_like(qc)
    return jnp.where(lane < HEAD_DIM, qc, zero), jnp.where(lane >= HEAD_DIM, qc, zero)


def _stack_heads(q_ref):
    tq, width = q_ref.shape
    parts = []
    for j in range(width // LANES):
        parts.extend(_split_heads(q_ref[:, j * LANES:(j + 1) * LANES]))
    return jnp.concatenate(parts, axis=0)


def _merge_heads(acc_lo, acc_hi):
    lane = lax.broadcasted_iota(I32, acc_lo.shape, 1)
    o_lo = acc_lo * (1.0 / acc_lo[:, HEAD_DIM:HEAD_DIM + 1])
    o_hi = acc_hi * (1.0 / acc_hi[:, 0:1])
    return jnp.where(lane < HEAD_DIM, o_lo, o_hi)


def _dsa_kernel(tq, tk, topk, s_len, qi_ref, iw_ref, qa_ref, kT_ref, v_ref, o_ref, sc_ref):
    i = pl.program_id(1)
    q0 = i * tq
    nkb = (q0 + tq + tk - 1) // tk
    n_sub = tk // LANES
    n_heads = 2 * N_PAIRS
    row = q0 + lax.broadcasted_iota(I32, (tq, tk), 0)
    col_iota = lax.broadcasted_iota(I32, (tq, tk), 1)

    w = iw_ref[...]
    wb = [jnp.broadcast_to(w[:, h:h + 1], (tq, LANES)) for h in range(n_heads)]
    q_idx = _stack_heads(qi_ref)

    def score_block(kb, carry):
        k0 = pl.multiple_of(kb * tk, tk)
        s = jnp.dot(q_idx, kT_ref[1, :, pl.ds(k0, tk)], preferred_element_type=F32)
        accs = [jnp.zeros((tq, LANES), F32)] * n_sub
        for h in range(n_heads):
            accs = [accs[c] + wb[h] * jnp.maximum(
                        s[h * tq:(h + 1) * tq, c * LANES:(c + 1) * LANES], 0.0)
                    for c in range(n_sub)]
        score = jnp.concatenate(accs, axis=1)
        sc_ref[:, pl.ds(k0, tk)] = jnp.where(col_iota + k0 <= row, score, -jnp.inf)
        return carry

    lax.fori_loop(0, nkb, score_block, 0)

    def fold_rows(step, init, reduce):
        def blk(kb, a):
            k0 = pl.multiple_of(kb * tk, tk)
            x = sc_ref[:, pl.ds(k0, tk)]
            for c in range(n_sub):
                a = step(a, x[:, c * LANES:(c + 1) * LANES], k0 + c * LANES)
            return a
        a = lax.fori_loop(0, nkb, blk, jnp.full((tq, LANES), init, F32))
        return reduce(a, axis=1, keepdims=True)

    def count(pred):
        return fold_rows(lambda a, x, c0: a + jnp.where(pred(x, c0), 1.0, 0.0), 0.0, jnp.sum)

    keep_all = (row[:, 0:1] + 1).astype(F32) <= topk
    row_max = fold_rows(lambda a, x, c0: jnp.maximum(a, x), -jnp.inf, jnp.max)
    row_min = fold_rows(lambda a, x, c0: jnp.minimum(a, jnp.where(x == -jnp.inf, jnp.inf, x)),
                        jnp.inf, jnp.min)

    RUN, DONE = 0.0, 1.0

    def probe(_, carry):
        lo, hi, state, thr = carry
        cand = lo + (hi - lo) * 0.5
        collapsed = (cand <= lo) | (cand >= hi)
        tot = count(lambda x, c0: x >= cand)
        running = state == RUN
        hit = running & (tot == topk)
        state = jnp.where(hit, DONE, jnp.where(running & collapsed, 2.0, state))
        thr = jnp.where(hit, cand, thr)
        return jnp.where(tot > topk, cand, lo), jnp.where(tot < topk, cand, hi), state, thr

    def n_running(state):
        return jnp.sum(jnp.where(state == RUN, 1.0, 0.0))

    def more_probes(carry):
        _, it, rest = carry
        rest = probe(0, probe(0, rest))
        return jnp.where(it + 2 >= MAX_PROBES, 0.0, n_running(rest[2])), it + 2, rest

    state0 = jnp.where(keep_all, DONE, RUN)
    hi0 = row_max + (jnp.abs(row_max) * 2.0 ** -20 + 1e-30)
    rest = lax.fori_loop(0, FIXED_PROBES, probe,
                         (row_min, hi0, state0, jnp.full((tq, 1), -F32_MAX, F32)))
    _, _, (_, _, state, thr) = lax.while_loop(
        lambda c: c[0] > 0.0, more_probes, (n_running(rest[2]), jnp.int32(FIXED_PROBES), rest))
    n_unresolved = jnp.sum(jnp.where(state == DONE, 0.0, 1.0))

    @pl.when(n_unresolved == 0.0)
    def _():
        def bias_block(kb, carry):
            k0 = pl.multiple_of(kb * tk, tk)
            sc_ref[:, pl.ds(k0, tk)] = jnp.where(sc_ref[:, pl.ds(k0, tk)] >= thr, 0.0, NEG)
            return carry
        lax.fori_loop(0, nkb, bias_block, 0)

    @pl.when(n_unresolved > 0.0)
    def _():
        def decode(code):
            bits = jnp.where(code < 0, code ^ jnp.int32(0x7FFFFFFF), code)
            return lax.bitcast_convert_type(bits, F32)

        def thr_step(j, code):
            cand = code + lax.shift_left(jnp.int32(1), 31 - j)
            cand_f = decode(cand)
            tot = count(lambda x, c0: x >= cand_f)
            return jnp.where(tot >= topk, cand, code)

        kth = decode(lax.fori_loop(0, 32, thr_step, jnp.full((tq, 1), INT_MIN, I32)))
        need = topk - count(lambda x, c0: x > kth)
        n_bits = max(1, int(math.ceil(math.log2(s_len))))
        lane_iota = lax.broadcasted_iota(I32, (tq, LANES), 1)

        def tie_step(j, p):
            cand = p + lax.shift_left(jnp.int32(1), n_bits - 1 - j)
            tot = count(lambda x, c0: (x == kth) & (lane_iota + c0 < cand))
            return jnp.where(tot < need, cand, p)

        tie_col = lax.fori_loop(0, n_bits, tie_step, jnp.zeros((tq, 1), I32))

        def bias_block(kb, carry):
            k0 = pl.multiple_of(kb * tk, tk)
            x = sc_ref[:, pl.ds(k0, tk)]
            col = col_iota + k0
            sel = (col <= row) & (keep_all | (x > kth) | ((x == kth) & (col <= tie_col)))
            sc_ref[:, pl.ds(k0, tk)] = jnp.where(sel, 0.0, NEG)
            return carry
        lax.fori_loop(0, nkb, bias_block, 0)

    _, acc = _flash_loop(
        _stack_heads(qa_ref),
        lambda k0: kT_ref[0, :, pl.ds(k0, tk)],
        lambda k0: v_ref[0, pl.ds(k0, tk), :],
        lambda k0: sc_ref[:, pl.ds(k0, tk)],
        0, nkb, tk, jnp.full((n_heads * tq, 1), NEG, F32), jnp.zeros((n_heads * tq, LANES), F32))
    lane = lax.broadcasted_iota(I32, (tq, LANES), 1)
    for j in range(N_PAIRS):
        o_even, o_odd = [
            a * (1.0 / a[:, HEAD_DIM:HEAD_DIM + 1])
            for a in (acc[2 * j * tq:(2 * j + 1) * tq], acc[(2 * j + 1) * tq:(2 * j + 2) * tq])]
        o_ref[:, j * LANES:(j + 1) * LANES] = jnp.where(
            lane < HEAD_DIM, o_even, pltpu.roll(o_odd, HEAD_DIM, 1))


def _dsa_attention(qi, iw, qa, kT, v, bsz, s_len, tq, tk):
    t_tokens = qi.shape[0]
    nq = s_len // tq
    topk = min(TOPK_MAX, s_len // 4)
    return pl.pallas_call(
        functools.partial(_dsa_kernel, tq, tk, float(topk), s_len),
        grid=(bsz, nq),
        in_specs=[
            pl.BlockSpec((tq, GROUP), lambda b, i: (b * nq + i, 0)),
            pl.BlockSpec((tq, LANES), lambda b, i: (b * nq + i, 0)),
            pl.BlockSpec((tq, GROUP), lambda b, i: (b * nq + i, 0)),
            pl.BlockSpec((2, LANES, s_len), lambda b, i: (0, 0, b)),
            pl.BlockSpec((1, s_len, LANES), lambda b, i: (0, b, 0)),
        ],
        out_specs=pl.BlockSpec((tq, GROUP), lambda b, i: (b * nq + i, 0)),
        out_shape=jax.ShapeDtypeStruct((t_tokens, GROUP), F32),
        scratch_shapes=[pltpu.VMEM((tq, s_len), F32)],
        compiler_params=_params("parallel", "parallel"),
        name="dsa_attention",
    )(qi, iw, qa, kT, v)


def _bias_tables(n, step, tq, tk, fn):
    d = np.arange(n)[:, None, None] * step + np.arange(tq)[None, :, None] - np.arange(tk)[None, None, :]
    return jnp.asarray(fn(d), F32)


def _window_kernel(tq, q_ref, kT_ref, v_ref, bias_ref, sink_ref, o_ref):
    j = pl.program_id(1)
    i = pl.program_id(2)
    q0 = i * tq
    wk = tq + B_WINDOW
    k0 = pl.multiple_of(jnp.maximum(q0 - B_WINDOW, 0), LANES)
    kT = kT_ref[0, :, pl.ds(k0, wk)]
    bias = bias_ref[(q0 - k0) // B_WINDOW]
    lane = lax.broadcasted_iota(I32, (tq, LANES), 1)
    s_both = jnp.dot(_stack_heads(q_ref), kT, preferred_element_type=F32)
    accs = []
    for half in range(2):
        sink = sink_ref[2 * j + half] * LOG2E
        s = s_both[half * tq:(half + 1) * tq] + bias
        m = jnp.maximum(jnp.max(s, axis=1, keepdims=True), sink)
        p = jnp.exp2(s - m).astype(BF16)
        acc = jnp.dot(p, v_ref[half, pl.ds(k0, wk), :], preferred_element_type=F32)
        accs.append(acc + jnp.where(lane == (HEAD_DIM if half == 0 else 0), jnp.exp2(sink - m), 0.0))
    o_ref[...] = _merge_heads(*accs)


def _window_attention(q, kT, v, sinks, bsz, s_len, tq):
    t_tokens = q.shape[0]
    nq = s_len // tq
    bias = _bias_tables(2, B_WINDOW, tq, tq + B_WINDOW,
                        lambda d: np.where((d >= 0) & (d < B_WINDOW), 0.0, NEG))
    return pl.pallas_call(
        functools.partial(_window_kernel, tq),
        grid=(bsz, N_PAIRS, nq),
        in_specs=[
            pl.BlockSpec((tq, LANES), lambda b, j, i: (b * nq + i, j)),
            pl.BlockSpec((1, LANES, s_len), lambda b, j, i: (2 + j // 2, 0, b)),
            pl.BlockSpec((2, s_len, LANES), lambda b, j, i: (1 + j // 2, b, 0)),
            pl.BlockSpec(bias.shape, lambda b, j, i: (0, 0, 0)),
            pl.BlockSpec(memory_space=pltpu.SMEM),
        ],
        out_specs=pl.BlockSpec((tq, LANES), lambda b, j, i: (b * nq + i, j)),
        out_shape=jax.ShapeDtypeStruct((t_tokens, GROUP), F32),
        compiler_params=_params("parallel", "parallel", "parallel"),
        name="window_attention",
    )(q, kT, v, bias, sinks)


def _dilated_log2_multiplicity(d):
    count = np.zeros(d.shape)
    for window, dilation in C_PATTERNS:
        count += (d >= 0) & (d <= window) & (d % dilation == 0)
    return np.where(count > 0, np.log2(np.maximum(count, 1)), NEG)


def _dilated_kernel(t, reach, q_ref, kT_ref, v_ref, bias_ref, o_ref):
    i = pl.program_id(2)
    q0 = i * t
    kb_lo = jnp.maximum(q0 - reach, 0) // t
    q_s = _stack_heads(q_ref)

    def body(kb, carry):
        m, acc_lo, acc_hi = carry
        k0 = pl.multiple_of(kb * t, t)
        s = jnp.dot(q_s, kT_ref[0, :, pl.ds(k0, t)], preferred_element_type=F32)
        m, alpha, p = _softmax_block(s, bias_ref[(q0 - k0) // t], m)
        acc_lo = alpha[0:t] * acc_lo + jnp.dot(p[0:t], v_ref[0, pl.ds(k0, t), :],
                                               preferred_element_type=F32)
        acc_hi = alpha[t:2 * t] * acc_hi + jnp.dot(p[t:2 * t], v_ref[1, pl.ds(k0, t), :],
                                                   preferred_element_type=F32)
        return m, acc_lo, acc_hi

    acc0 = jnp.zeros((t, LANES), F32)
    _, acc_lo, acc_hi = lax.fori_loop(kb_lo, i + 1, body,
                                      (jnp.full((2 * t, 1), NEG, F32), acc0, acc0))
    o_ref[...] = _merge_heads(acc_lo, acc_hi)


def _dilated_attention(q, kT, v, bsz, s_len, t):
    t_tokens = q.shape[0]
    nq = s_len // t
    reach = max(wd for wd, _ in C_PATTERNS)
    bias = _bias_tables((reach + t - 1) // t + 1, t, t, t, _dilated_log2_multiplicity)
    return pl.pallas_call(
        functools.partial(_dilated_kernel, t, reach),
        grid=(bsz, N_PAIRS, nq),
        in_specs=[
            pl.BlockSpec((t, LANES), lambda b, j, i: (b * nq + i, j)),
            pl.BlockSpec((1, LANES, s_len), lambda b, j, i: (j, 0, b)),
            pl.BlockSpec((2, s_len, LANES), lambda b, j, i: (j, b, 0)),
            pl.BlockSpec(bias.shape, lambda b, j, i: (0, 0, 0)),
        ],
        out_specs=pl.BlockSpec((t, LANES), lambda b, j, i: (b * nq + i, j)),
        out_shape=jax.ShapeDtypeStruct((t_tokens, GROUP), F32),
        compiler_params=_params("parallel", "parallel", "parallel"),
        name="dilated_attention",
    )(q, kT, v, bias)


def _diff_kernel(tq, tk, lambda_init, q_ref, kT_ref, v_ref, bias_ref, lq1_ref, lk1_ref, lq2_ref,
                 lk2_ref, sg_ref, o_ref):
    i = pl.program_id(2)
    n_diag = tq // tk
    q_s = _stack_heads(q_ref)
    kT_of = lambda k0: kT_ref[0, :, pl.ds(k0, tk)]
    v_of = lambda k0: v_ref[0, pl.ds(k0, tk), :]
    m0 = jnp.full((2 * tq, 1), NEG, F32)
    acc0 = jnp.zeros((2 * tq, 2 * LANES), F32)
    carry = _flash_loop(q_s, kT_of, v_of, lambda k0: None, 0, i * n_diag, tk, m0, acc0)
    for d in range(n_diag):
        k0 = pl.multiple_of(i * tq + d * tk, tk)
        carry = _flash_block(q_s, kT_of(k0), v_of(k0), bias_ref[d], *carry)
    acc = carry[1]
    outs = [a[:, 0:LANES] * (1.0 / a[:, LANES:LANES + 1]) for a in (acc[0:tq], acc[tq:2 * tq])]
    lam = (jnp.exp(jnp.sum(lq1_ref[...] * lk1_ref[...], axis=1, keepdims=True))
           - jnp.exp(jnp.sum(lq2_ref[...] * lk2_ref[...], axis=1, keepdims=True)) + lambda_init)
    o = outs[0] - lam * outs[1]
    ms = jnp.mean(o * o, axis=-1, keepdims=True)
    o_ref[...] = (o * lax.rsqrt(ms + NORM_EPS) * sg_ref[...]) * (1.0 - lambda_init)


def _diff_attention(q, kT, v, lq1, lk1, lq2, lk2, sub_gain, lambda_init, k_base, bsz, s_len, tq, tk):
    t_tokens = q.shape[0]
    nq = s_len // tq
    bias = _bias_tables(tq // tk, -tk, tq, tk, lambda d: np.where(d >= 0, 0.0, NEG))
    vec = lambda a: a.reshape(1, -1).astype(F32)
    small = lambda n: pl.BlockSpec((1, n), lambda b, j, i: (0, 0))
    return pl.pallas_call(
        functools.partial(_diff_kernel, tq, tk, lambda_init),
        grid=(bsz, N_PAIRS, nq),
        in_specs=[
            pl.BlockSpec((tq, LANES), lambda b, j, i: (b * nq + i, j)),
            pl.BlockSpec((1, LANES, s_len), lambda b, j, i: (k_base + j, 0, b)),
            pl.BlockSpec((1, s_len, 2 * LANES), lambda b, j, i: (j, b, 0)),
            pl.BlockSpec(bias.shape, lambda b, j, i: (0, 0, 0)),
            small(HEAD_DIM), small(HEAD_DIM), small(HEAD_DIM), small(HEAD_DIM), small(LANES),
        ],
        out_specs=pl.BlockSpec((tq, LANES), lambda b, j, i: (b * nq + i, j)),
        out_shape=jax.ShapeDtypeStruct((t_tokens, GROUP), F32),
        compiler_params=_params("parallel", "parallel", "parallel"),
        name="diff_attention",
    )(q, kT, v, bias, vec(lq1), vec(lk1), vec(lq2), vec(lk2), vec(sub_gain))


def _outproj_kernel(o1_ref, o2_ref, g_ref, x_ref, p_ref, wo_ref, pg_ref, wg_ref, wp_ref, out_ref):
    g = g_ref[...]
    y1 = (o1_ref[...] * g[:, 0:GROUP]).astype(BF16)
    y2 = (o2_ref[...] * g[:, GROUP:2 * GROUP]).astype(BF16)
    x1 = (x_ref[...]
          + jnp.dot(y1, wo_ref[0:GROUP, :], preferred_element_type=F32)
          + jnp.dot(y2, wo_ref[GROUP:2 * GROUP, :], preferred_element_type=F32))
    ms = jnp.mean(x1 * x1, axis=-1, keepdims=True)
    hn = (x1 * lax.rsqrt(ms + NORM_EPS) * pg_ref[...]).astype(BF16)
    z = jnp.dot(hn, wg_ref[...], preferred_element_type=F32)
    gate = 1.0 / (1.0 + jnp.exp(-z))
    pp = jnp.dot(p_ref[...].astype(BF16), wp_ref[...], preferred_element_type=F32)
    out_ref[...] = x1 + pp * gate


def _outproj(o1, o2, gate, x2, p2, w_out, ple_gain, w_gate, w_proj, tm):
    t_tokens = x2.shape[0]
    tok = lambda n: pl.BlockSpec((tm, n), lambda i: (i, 0))
    full = lambda a: pl.BlockSpec(a.shape, lambda i: (0, 0))
    pg = ple_gain.reshape(1, D_MODEL)
    return pl.pallas_call(
        _outproj_kernel,
        grid=(t_tokens // tm,),
        in_specs=[tok(GROUP), tok(GROUP), tok(2 * GROUP), tok(D_MODEL), tok(PLE_DIM),
                  full(w_out), full(pg), full(w_gate), full(w_proj)],
        out_specs=tok(D_MODEL),
        out_shape=jax.ShapeDtypeStruct((t_tokens, D_MODEL), F32),
        compiler_params=_params("parallel"),
        name="outproj_ple",
    )(o1, o2, gate, x2, p2, w_out, pg, w_gate, w_proj)


TM = 256
TQ_DSA = 128
TK_DSA = 512
TQ_WINDOW = 256
T_DILATED = 512
TQ_DIFF = 512
TK_DIFF = 512


def _rope_tables(s_len):
    half = HEAD_DIM // 2
    inv = ROPE_THETA ** (-jnp.arange(half, dtype=F32) / half)
    ang = jnp.arange(s_len).astype(F32)[:, None] * inv[None, :]
    cos = jnp.tile(jnp.cos(ang), (1, 4))
    sin = jnp.sin(ang)
    sin_signed = jnp.tile(jnp.concatenate([-sin, sin], axis=1), (1, 2))
    return cos, sin_signed


def _pair_gain(g):
    return jnp.concatenate([g, g]).astype(F32)


def _even_layer(x2, p2, bsz, s_len, tables, norm_gain, w_in, w_out, a_q_gain, a_k_gain, idx_k_gain,
                b_q_gain, b_k_gain, b_sinks, ple_gain, w_gate, w_proj):
    hd = HEAD_DIM
    offs = np.cumsum([0, 512, hd, hd, 512, hd, 8, 512, 512, 2 * hd, 2 * hd, 512])
    aq, ak, av, iq, ik, iw, ag, bq, bk, bv, bg = [w_in[:, offs[n]:offs[n + 1]] for n in range(11)]
    dup = lambda c: jnp.concatenate([c, c], axis=1)
    w = jnp.concatenate(
        [aq, iq, bq, ag, bg,
         dup(ak), dup(ik), dup(bk[:, :hd]), dup(bk[:, hd:]),
         dup(av), dup(bv[:, :hd]), dup(bv[:, hd:]),
         jnp.pad(iw, ((0, 0), (0, LANES - 8)))], axis=1).astype(BF16)
    ones = jnp.ones((LANES,), F32)
    gains = jnp.stack(
        [_pair_gain(a_q_gain)] * 4 + [ones] * 4 + [_pair_gain(b_q_gain)] * 4 + [ones] * 8
        + [_pair_gain(a_k_gain), _pair_gain(idx_k_gain), _pair_gain(b_k_gain), _pair_gain(b_k_gain)]
        + [ones] * 4)
    idx_scale = (8 * hd) ** -0.5
    plan = ([("q", 0, c, True, QSCALE) for c in range(4)]
            + [("q", 1, c, False, 1.0) for c in range(4)]
            + [("q", 2, c, True, QSCALE) for c in range(4)]
            + [("silu", 3, c) for c in range(8)]
            + [("k", 4, r) for r in range(4)]
            + [("v2", 5, 2 * r) for r in range(3)]
            + [("iw", 6, idx_scale)])
    out_defs = [("tok", 512, BF16), ("tok", 512, BF16), ("tok", 512, BF16), ("tok", 1024, F32),
                ("kt", 4, BF16), ("v", 6, BF16), ("tok", LANES, F32)]
    qa, qi, qb, gate, kT, v, iwt = _inproj(x2, norm_gain, w, gains, *tables, plan, out_defs, TM)
    oa = _dsa_attention(qi, iwt, qa, kT, v, bsz, s_len, min(TQ_DSA, s_len), TK_DSA)
    ob = _window_attention(qb, kT, v, b_sinks.astype(F32), bsz, s_len, TQ_WINDOW)
    return _outproj(oa, ob, gate, x2, p2, w_out.astype(BF16), ple_gain, w_gate.astype(BF16),
                    w_proj.astype(BF16), TM)


def _odd_layer(x2, p2, bsz, s_len, tables, norm_gain, w_in, w_out, c_q_gain, c_k_gain, d_q_gain,
               d_k_gain, lq1, lk1, lq2, lk2, sub_gain, lambda_init, ple_gain, w_gate, w_proj):
    ones = jnp.ones((LANES,), F32)
    gains = jnp.stack([_pair_gain(c_q_gain)] * 4 + [_pair_gain(c_k_gain)] * 4 + [ones] * 8
                      + [_pair_gain(d_q_gain)] * 4 + [_pair_gain(d_k_gain)] * 4 + [ones] * 8)
    plan = ([("q", 0, c, True, QSCALE) for c in range(4)]
            + [("k", 3, r) for r in range(4)]
            + [("v2", 4, 2 * r) for r in range(4)]
            + [("silu", 2, c) for c in range(4)]
            + [("q", 1, c, True, QSCALE) for c in range(4)]
            + [("k", 3, 4 + r) for r in range(4)]
            + [("v1", 5, r) for r in range(4)]
            + [("silu", 2, 4 + c) for c in range(4)])
    out_defs = [("tok", 512, BF16), ("tok", 512, BF16), ("tok", 1024, F32),
                ("kt", 8, BF16), ("v", 8, BF16), ("vw", 4, BF16)]
    qc, qd, gate, kT, vc, vd = _inproj(x2, norm_gain, w_in.astype(BF16), gains, *tables, plan,
                                       out_defs, TM)
    oc = _dilated_attention(qc, kT, vc, bsz, s_len, T_DILATED)
    od = _diff_attention(qd, kT, vd, lq1, lk1, lq2, lk2, sub_gain, lambda_init, N_PAIRS, bsz, s_len,
                         TQ_DIFF, TK_DIFF)
    return _outproj(oc, od, gate, x2, p2, w_out.astype(BF16), ple_gain, w_gate.astype(BF16),
                    w_proj.astype(BF16), TM)


def kernel(x, p, norm_gain, w_in_even, w_out_even, a_q_gain, a_k_gain, idx_k_gain, b_q_gain, b_k_gain, b_sinks, w_in_odd, w_out_odd, c_q_gain, c_k_gain, d_q_gain, d_k_gain, d_lambda_q1, d_lambda_k1, d_lambda_q2, d_lambda_k2, d_subln_gain, ple_norm_gain, w_ple_gate, w_ple_proj):
    bsz, s_len, d_model = x.shape
    depth = p.shape[0]
    assert d_model == D_MODEL and s_len % 512 == 0
    tables = _rope_tables(s_len)
    x2 = x.reshape(bsz * s_len, d_model)
    for i in range(depth):
        j = i // 2
        p2 = p[i].reshape(bsz * s_len, PLE_DIM)
        if i % 2 == 0:
            x2 = _even_layer(x2, p2, bsz, s_len, tables, norm_gain[i], w_in_even[j], w_out_even[j],
                             a_q_gain[j], a_k_gain[j], idx_k_gain[j], b_q_gain[j], b_k_gain[j],
                             b_sinks[j], ple_norm_gain[i], w_ple_gate[i], w_ple_proj[i])
        else:
            lambda_init = 0.8 - 0.6 * math.exp(-0.3 * i)
            x2 = _odd_layer(x2, p2, bsz, s_len, tables, norm_gain[i], w_in_odd[j], w_out_odd[j],
                            c_q_gain[j], c_k_gain[j], d_q_gain[j], d_k_gain[j], d_lambda_q1[j],
                            d_lambda_k1[j], d_lambda_q2[j], d_lambda_k2[j], d_subln_gain[j],
                            lambda_init, ple_norm_gain[i], w_ple_gate[i], w_ple_proj[i])
    return x2.reshape(bsz, s_len, d_model)
```

```python
import functools
import math

import numpy as np
import jax
import jax.numpy as jnp
from jax import lax
from jax.experimental import pallas as pl
from jax.experimental.pallas import tpu as pltpu

F32 = jnp.float32
BF16 = jnp.bfloat16
I32 = jnp.int32

D_MODEL = 1024
HEAD_DIM = 64
ROPE_THETA = 10000.0
NORM_EPS = 1e-6
PLE_DIM = 256
TOPK_MAX = 256
B_WINDOW = 128
C_PATTERNS = ((128, 1), (512, 4), (2048, 16))
N_PAIRS = 4
LANES = 128
GROUP = 512
VT_PAD = 16
LOG2E = 1.4426950408889634
QSCALE = HEAD_DIM ** -0.5 * LOG2E
NEG = -1e30
INT_MIN = -(2 ** 31)
F32_MAX = float(np.finfo(np.float32).max)
FIXED_PROBES = 16
MAX_PROBES = 40
VMEM_LIMIT = 48 * 1024 * 1024


def _params(*sem):
    return pltpu.CompilerParams(dimension_semantics=sem, vmem_limit_bytes=VMEM_LIMIT)


def _norm_rope(y, gain, cos, sin_signed, bd, first_half, norm):
    if norm:
        ss = y * y
        hi = ss.astype(BF16)
        lo = (ss - hi.astype(F32)).astype(BF16)
        seg = (jnp.dot(hi, bd, preferred_element_type=F32)
               + jnp.dot(lo, bd, preferred_element_type=F32))
        y = y * lax.rsqrt(seg * (1.0 / HEAD_DIM) + NORM_EPS) * gain
    rot = jnp.where(first_half, pltpu.roll(y, 96, 1), pltpu.roll(y, 32, 1))
    return y * cos + rot * sin_signed


def _inproj_kernel(plan, x_ref, ng_ref, w_ref, cos_ref, sin_ref, gains_ref, bd_ref, *outs):
    x = x_ref[...]
    ms = jnp.mean(x * x, axis=-1, keepdims=True)
    h = (x * lax.rsqrt(ms + NORM_EPS) * ng_ref[...]).astype(BF16)
    cos = cos_ref[...]
    sin = sin_ref[...]
    bd = bd_ref[...]
    tm = x.shape[0]
    lane = lax.broadcasted_iota(I32, (tm, LANES), 1)
    lo_half = lane < HEAD_DIM
    first_half = (lane & (HEAD_DIM - 1)) < HEAD_DIM // 2
    one_at_64 = jnp.where(lane == HEAD_DIM, 1.0, 0.0)
    one_at_0 = jnp.where(lane == 0, 1.0, 0.0)
    for g in range(len(plan) // 4):
        yg = jnp.dot(h, w_ref[:, g * GROUP:(g + 1) * GROUP], preferred_element_type=F32)
        for c4 in range(4):
            c = g * 4 + c4
            y = yg[:, c4 * LANES:(c4 + 1) * LANES]
            step = plan[c]
            kind, out = step[0], outs[step[1]]
            if kind == "q":
                _, _, col, norm, mult = step
                y = _norm_rope(y, gains_ref[c:c + 1, :], cos, sin, bd, first_half, norm)
                out[:, col * LANES:(col + 1) * LANES] = (y * mult).astype(BF16)
            elif kind == "k":
                _, _, row, mult = step
                y = _norm_rope(y, gains_ref[c:c + 1, :], cos, sin, bd, first_half, True)
                out[row] = (y * mult).T.astype(BF16)
            elif kind == "vt":
                _, _, row, head_dim, n_heads = step
                yT = y.T.astype(BF16)
                sub = lax.broadcasted_iota(I32, (VT_PAD, tm), 0)
                ones_row = jnp.where(sub == 0, 1.0, 0.0).astype(BF16)
                for n in range(n_heads):
                    base = n * (head_dim + VT_PAD)
                    out[row, base:base + head_dim, :] = yT[n * head_dim:(n + 1) * head_dim]
                    out[row, base + head_dim:base + head_dim + VT_PAD, :] = ones_row
            elif kind == "silu":
                _, _, col = step
                out[:, col * LANES:(col + 1) * LANES] = y * (1.0 / (1.0 + jnp.exp(-y)))
            elif kind == "v2":
                _, _, row = step
                out[row] = jnp.where(lo_half, y, one_at_64).astype(BF16)
                out[row + 1] = jnp.where(lo_half, one_at_0, y).astype(BF16)
            elif kind == "v1":
                _, _, row = step
                out[row, :, 0:LANES] = y.astype(BF16)
                out[row, :, LANES:2 * LANES] = one_at_0.astype(BF16)
            elif kind == "iw":
                _, _, mult = step
                out[...] = y * mult
            else:
                raise ValueError(kind)


def _inproj(x2, norm_gain, w, gains, cos_t, sin_t, plan, out_defs, tm):
    t_tokens = x2.shape[0]
    s_len = cos_t.shape[0]
    n_cols = w.shape[1]
    n_tiles = t_tokens // tm
    s_tiles = s_len // tm
    bd = jnp.asarray(np.kron(np.eye(2), np.ones((HEAD_DIM, HEAD_DIM))), BF16)
    out_shapes, out_specs = [], []
    for kind, n, dtype in out_defs:
        if kind == "tok":
            out_shapes.append(jax.ShapeDtypeStruct((t_tokens, n), dtype))
            out_specs.append(pl.BlockSpec((tm, n), lambda i: (i, 0)))
        elif kind == "kt":
            out_shapes.append(jax.ShapeDtypeStruct((n, LANES, t_tokens), dtype))
            out_specs.append(pl.BlockSpec((n, LANES, tm), lambda i: (0, 0, i)))
        elif kind == "v":
            out_shapes.append(jax.ShapeDtypeStruct((n, t_tokens, LANES), dtype))
            out_specs.append(pl.BlockSpec((n, tm, LANES), lambda i: (0, i, 0)))
        elif kind == "vt":
            out_shapes.append(jax.ShapeDtypeStruct((n[0], n[1], t_tokens), dtype))
            out_specs.append(pl.BlockSpec((n[0], n[1], tm), lambda i: (0, 0, i)))
        elif kind == "vw":
            out_shapes.append(jax.ShapeDtypeStruct((n, t_tokens, 2 * LANES), dtype))
            out_specs.append(pl.BlockSpec((n, tm, 2 * LANES), lambda i: (0, i, 0)))
    return pl.pallas_call(
        functools.partial(_inproj_kernel, plan),
        grid=(n_tiles,),
        in_specs=[
            pl.BlockSpec((tm, D_MODEL), lambda i: (i, 0)),
            pl.BlockSpec((1, D_MODEL), lambda i: (0, 0)),
            pl.BlockSpec((D_MODEL, n_cols), lambda i: (0, 0)),
            pl.BlockSpec((tm, LANES), lambda i: (i % s_tiles, 0)),
            pl.BlockSpec((tm, LANES), lambda i: (i % s_tiles, 0)),
            pl.BlockSpec(gains.shape, lambda i: (0, 0)),
            pl.BlockSpec((LANES, LANES), lambda i: (0, 0)),
        ],
        out_specs=out_specs,
        out_shape=out_shapes,
        compiler_params=_params("parallel"),
        name="inproj",
    )(x2, norm_gain.reshape(1, D_MODEL), w, cos_t, sin_t, gains, bd)


def _softmax_block(s, bias, m):
    if bias is not None:
        tq, tk = bias.shape
        s = (s.reshape(s.shape[0] // tq, tq, tk) + bias[None]).reshape(s.shape)
    m_new = jnp.maximum(m, jnp.max(s, axis=1, keepdims=True))
    return m_new, jnp.exp2(m - m_new), jnp.exp2(s - m_new).astype(BF16)


def _flash_block(q_s, kT, v, bias, m, acc):
    s = jnp.dot(q_s, kT, preferred_element_type=F32)
    m_new, alpha, p = _softmax_block(s, bias, m)
    return m_new, alpha * acc + jnp.dot(p, v, preferred_element_type=F32)


def _flash_loop(q_s, kT_of, v_of, bias_of, lo, hi, tk, m, acc, group=1):
    def body(it, carry):
        k0s = [pl.multiple_of((lo + it * group + g) * tk, tk) for g in range(group)]
        logits = [jnp.dot(q_s, kT_of(k0), preferred_element_type=F32) for k0 in k0s]
        m, acc = carry
        for s, k0 in zip(logits, k0s):
            m, alpha, p = _softmax_block(s, bias_of(k0), m)
            acc = alpha * acc + jnp.dot(p, v_of(k0), preferred_element_type=F32)
        return m, acc
    return lax.fori_loop(0, (hi - lo) // group, body, (m, acc))


def _split_heads(qc):
    lane = lax.broadcasted_iota(I32, qc.shape, 1)
    zero = jnp.zeros_like(qc)
    return jnp.where(lane < HEAD_DIM, qc, zero), jnp.where(lane >= HEAD_DIM, qc, zero)


def _stack_heads(q_ref):
    tq, width = q_ref.shape
    parts = []
    for j in range(width // LANES):
        parts.extend(_split_heads(q_ref[:, j * LANES:(j + 1) * LANES]))
    return jnp.concatenate(parts, axis=0)


def _merge_heads(acc_lo, acc_hi):
    lane = lax.broadcasted_iota(I32, acc_lo.shape, 1)
    o_lo = acc_lo * (1.0 / acc_lo[:, HEAD_DIM:HEAD_DIM + 1])
    o_hi = acc_hi * (1.0 / acc_hi[:, 0:1])
    return jnp.where(lane < HEAD_DIM, o_lo, o_hi)


def _dsa_kernel(tq, tk, topk, s_len, qi_ref, iw_ref, kiT_ref, qaT_ref, ka_ref, vaT_ref, o_ref,
                sc_ref, *flash_scratch):
    i = pl.program_id(1)
    q0 = i * tq
    nkb = (q0 + tq + tk - 1) // tk
    n_sub = tk // LANES
    n_heads = 2 * N_PAIRS
    row = q0 + lax.broadcasted_iota(I32, (tq, tk), 0)
    col_iota = lax.broadcasted_iota(I32, (tq, tk), 1)

    w = iw_ref[...]
    wb = [jnp.broadcast_to(w[:, h:h + 1], (tq, LANES)) for h in range(n_heads)]
    q_idx = _stack_heads(qi_ref)

    def score_block(kb, carry):
        k0 = pl.multiple_of(kb * tk, tk)
        s = jnp.dot(q_idx, kiT_ref[0, :, pl.ds(k0, tk)], preferred_element_type=F32)
        accs = [jnp.zeros((tq, LANES), F32)] * n_sub
        for h in range(n_heads):
            accs = [accs[c] + wb[h] * jnp.maximum(
                        s[h * tq:(h + 1) * tq, c * LANES:(c + 1) * LANES], 0.0)
                    for c in range(n_sub)]
        score = jnp.concatenate(accs, axis=1)
        sc_ref[:, pl.ds(k0, tk)] = jnp.where(col_iota + k0 <= row, score, -jnp.inf)
        return carry

    lax.fori_loop(0, nkb, score_block, 0)

    def fold_rows(step, init, reduce):
        def blk(kb, a):
            k0 = pl.multiple_of(kb * tk, tk)
            x = sc_ref[:, pl.ds(k0, tk)]
            for c in range(n_sub):
                a = step(a, x[:, c * LANES:(c + 1) * LANES], k0 + c * LANES)
            return a
        a = lax.fori_loop(0, nkb, blk, jnp.full((tq, LANES), init, F32))
        return reduce(a, axis=1, keepdims=True)

    def count(pred):
        return fold_rows(lambda a, x, c0: a + jnp.where(pred(x, c0), 1.0, 0.0), 0.0, jnp.sum)

    keep_all = (row[:, 0:1] + 1).astype(F32) <= topk
    row_max = fold_rows(lambda a, x, c0: jnp.maximum(a, x), -jnp.inf, jnp.max)
    row_min = fold_rows(lambda a, x, c0: jnp.minimum(a, jnp.where(x == -jnp.inf, jnp.inf, x)),
                        jnp.inf, jnp.min)

    RUN, DONE = 0.0, 1.0

    def probe(_, carry):
        lo, hi, state, thr = carry
        cand = lo + (hi - lo) * 0.5
        collapsed = (cand <= lo) | (cand >= hi)
        tot = count(lambda x, c0: x >= cand)
        running = state == RUN
        hit = running & (tot == topk)
        state = jnp.where(hit, DONE, jnp.where(running & collapsed, 2.0, state))
        thr = jnp.where(hit, cand, thr)
        return jnp.where(tot > topk, cand, lo), jnp.where(tot < topk, cand, hi), state, thr

    def n_running(state):
        return jnp.sum(jnp.where(state == RUN, 1.0, 0.0))

    def more_probes(carry):
        _, it, rest = carry
        rest = probe(0, probe(0, rest))
        return jnp.where(it + 2 >= MAX_PROBES, 0.0, n_running(rest[2])), it + 2, rest

    state0 = jnp.where(keep_all, DONE, RUN)
    hi0 = row_max + (jnp.abs(row_max) * 2.0 ** -20 + 1e-30)
    rest = lax.fori_loop(0, FIXED_PROBES, probe,
                         (row_min, hi0, state0, jnp.full((tq, 1), -F32_MAX, F32)))
    _, _, (_, _, state, thr) = lax.while_loop(
        lambda c: c[0] > 0.0, more_probes, (n_running(rest[2]), jnp.int32(FIXED_PROBES), rest))
    n_unresolved = jnp.sum(jnp.where(state == DONE, 0.0, 1.0))

    @pl.when(n_unresolved == 0.0)
    def _():
        def bias_block(kb, carry):
            k0 = pl.multiple_of(kb * tk, tk)
            sc_ref[:, pl.ds(k0, tk)] = jnp.where(sc_ref[:, pl.ds(k0, tk)] >= thr, 0.0, NEG)
            return carry
        lax.fori_loop(0, nkb, bias_block, 0)

    @pl.when(n_unresolved > 0.0)
    def _():
        def decode(code):
            bits = jnp.where(code < 0, code ^ jnp.int32(0x7FFFFFFF), code)
            return lax.bitcast_convert_type(bits, F32)

        def thr_step(j, code):
            cand = code + lax.shift_left(jnp.int32(1), 31 - j)
            cand_f = decode(cand)
            tot = count(lambda x, c0: x >= cand_f)
            return jnp.where(tot >= topk, cand, code)

        kth = decode(lax.fori_loop(0, 32, thr_step, jnp.full((tq, 1), INT_MIN, I32)))
        need = topk - count(lambda x, c0: x > kth)
        n_bits = max(1, int(math.ceil(math.log2(s_len))))
        lane_iota = lax.broadcasted_iota(I32, (tq, LANES), 1)

        def tie_step(j, p):
            cand = p + lax.shift_left(jnp.int32(1), n_bits - 1 - j)
            tot = count(lambda x, c0: (x == kth) & (lane_iota + c0 < cand))
            return jnp.where(tot < need, cand, p)

        tie_col = lax.fori_loop(0, n_bits, tie_step, jnp.zeros((tq, 1), I32))

        def bias_block(kb, carry):
            k0 = pl.multiple_of(kb * tk, tk)
            x = sc_ref[:, pl.ds(k0, tk)]
            col = col_iota + k0
            sel = (col <= row) & (keep_all | (x > kth) | ((x == kth) & (col <= tie_col)))
            sc_ref[:, pl.ds(k0, tk)] = jnp.where(sel, 0.0, NEG)
            return carry
        lax.fori_loop(0, nkb, bias_block, 0)

    qT_s = jnp.concatenate([_split_heads_t(qaT_ref[j]) for j in range(N_PAIRS)], axis=1)

    def bias_of(kb, last):
        return sc_ref[:, pl.ds(pl.multiple_of(kb * tk, tk), tk)].T

    _skewed_flash_t(0, nkb, tk, qT_s, ka_ref, vaT_ref, bias_of, *flash_scratch)
    acc = flash_scratch[3][...]
    o = acc[0:HEAD_DIM] * (1.0 / acc[HEAD_DIM:HEAD_DIM + 1])
    for j in range(N_PAIRS):
        pair = jnp.concatenate([o[:, 2 * j * tq:(2 * j + 1) * tq],
                                o[:, (2 * j + 1) * tq:(2 * j + 2) * tq]], axis=0)
        o_ref[:, j * LANES:(j + 1) * LANES] = pair.T


def _dsa_attention(qi, iw, kiT, qaT, ka, vaT, bsz, s_len, tq, tk):
    t_tokens = qi.shape[0]
    nq = s_len // tq
    topk = min(TOPK_MAX, s_len // 4)
    return pl.pallas_call(
        functools.partial(_dsa_kernel, tq, tk, float(topk), s_len),
        grid=(bsz, nq),
        in_specs=[
            pl.BlockSpec((tq, GROUP), lambda b, i: (b * nq + i, 0)),
            pl.BlockSpec((tq, LANES), lambda b, i: (b * nq + i, 0)),
            pl.BlockSpec((1, LANES, s_len), lambda b, i: (0, 0, b)),
            pl.BlockSpec((N_PAIRS, LANES, tq), lambda b, i: (0, 0, b * nq + i)),
            pl.BlockSpec((s_len, LANES), lambda b, i: (b, 0)),
            pl.BlockSpec((1, vaT.shape[1], s_len), lambda b, i: (0, 0, b)),
        ],
        out_specs=pl.BlockSpec((tq, GROUP), lambda b, i: (b * nq + i, 0)),
        out_shape=jax.ShapeDtypeStruct((t_tokens, GROUP), F32),
        scratch_shapes=[pltpu.VMEM((tq, s_len), F32)]
        + _flash_t_scratch(tk, 2 * N_PAIRS * tq, vaT.shape[1]),
        compiler_params=_params("parallel", "parallel"),
        name="dsa_attention",
    )(qi, iw, kiT, qaT, ka, vaT)


def _bias_tables(n, step, tq, tk, fn):
    d = np.arange(n)[:, None, None] * step + np.arange(tq)[None, :, None] - np.arange(tk)[None, None, :]
    return jnp.asarray(fn(d), F32)


def _window_kernel(tq, q_ref, kT_ref, v_ref, bias_ref, sink_ref, o_ref):
    j = pl.program_id(1)
    i = pl.program_id(2)
    q0 = i * tq
    wk = tq + B_WINDOW
    k0 = pl.multiple_of(jnp.maximum(q0 - B_WINDOW, 0), LANES)
    kT = kT_ref[0, :, pl.ds(k0, wk)]
    bias = bias_ref[(q0 - k0) // B_WINDOW]
    lane = lax.broadcasted_iota(I32, (tq, LANES), 1)
    s_both = jnp.dot(_stack_heads(q_ref), kT, preferred_element_type=F32)
    accs = []
    for half in range(2):
        sink = sink_ref[2 * j + half] * LOG2E
        s = s_both[half * tq:(half + 1) * tq] + bias
        m = jnp.maximum(jnp.max(s, axis=1, keepdims=True), sink)
        p = jnp.exp2(s - m).astype(BF16)
        acc = jnp.dot(p, v_ref[half, pl.ds(k0, wk), :], preferred_element_type=F32)
        accs.append(acc + jnp.where(lane == (HEAD_DIM if half == 0 else 0), jnp.exp2(sink - m), 0.0))
    o_ref[...] = _merge_heads(*accs)


def _window_attention(q, kT, v, sinks, bsz, s_len, tq):
    t_tokens = q.shape[0]
    nq = s_len // tq
    bias = _bias_tables(2, B_WINDOW, tq, tq + B_WINDOW,
                        lambda d: np.where((d >= 0) & (d < B_WINDOW), 0.0, NEG))
    return pl.pallas_call(
        functools.partial(_window_kernel, tq),
        grid=(bsz, N_PAIRS, nq),
        in_specs=[
            pl.BlockSpec((tq, LANES), lambda b, j, i: (b * nq + i, j)),
            pl.BlockSpec((1, LANES, s_len), lambda b, j, i: (1 + j // 2, 0, b)),
            pl.BlockSpec((2, s_len, LANES), lambda b, j, i: (j // 2, b, 0)),
            pl.BlockSpec(bias.shape, lambda b, j, i: (0, 0, 0)),
            pl.BlockSpec(memory_space=pltpu.SMEM),
        ],
        out_specs=pl.BlockSpec((tq, LANES), lambda b, j, i: (b * nq + i, j)),
        out_shape=jax.ShapeDtypeStruct((t_tokens, GROUP), F32),
        compiler_params=_params("parallel", "parallel", "parallel"),
        name="window_attention",
    )(q, kT, v, bias, sinks)


def _dilated_log2_multiplicity(d):
    count = np.zeros(d.shape)
    for window, dilation in C_PATTERNS:
        count += (d >= 0) & (d <= window) & (d % dilation == 0)
    return np.where(count > 0, np.log2(np.maximum(count, 1)), NEG)


def _split_heads_t(qT):
    row = lax.broadcasted_iota(I32, qT.shape, 0)
    zero = jnp.zeros_like(qT)
    return jnp.concatenate([jnp.where(row < HEAD_DIM, qT, zero), jnp.where(row >= HEAD_DIM, qT, zero)],
                           axis=1)


def _skewed_flash_t(lo, hi, tk, qT_s, k_ref, vT_ref, bias_of, sa_ref, sb_ref, m_ref, acc_ref):
    m_ref[...] = jnp.full(m_ref.shape, NEG, F32)
    acc_ref[...] = jnp.zeros(acc_ref.shape, F32)

    def logits_into(dst, kb):
        dst[...] = jnp.dot(k_ref[pl.ds(pl.multiple_of(kb * tk, tk), tk), :], qT_s,
                           preferred_element_type=F32)

    def update_from(src, kb, last=False):
        s = src[...]
        bias = bias_of(kb, last)
        if bias is not None:
            w = bias.shape[1]
            s = jnp.concatenate([s[:, r * w:(r + 1) * w] + bias for r in range(s.shape[1] // w)],
                                axis=1)
        m = m_ref[...]
        m_new = jnp.maximum(m, jnp.max(s, axis=0, keepdims=True))
        p = jnp.exp2(s - m_new).astype(BF16)
        vT = vT_ref[0, :, pl.ds(pl.multiple_of(kb * tk, tk), tk)]
        acc_ref[...] = jnp.exp2(m - m_new) * acc_ref[...] + jnp.dot(vT, p, preferred_element_type=F32)
        m_ref[...] = m_new

    n = hi - lo
    logits_into(sa_ref, lo)

    def pair(pp, carry):
        kb = lo + 2 * pp
        logits_into(sb_ref, kb + 1)
        update_from(sa_ref, kb)
        logits_into(sa_ref, kb + 2)
        update_from(sb_ref, kb + 1)
        return carry

    lax.fori_loop(0, (n - 1) // 2, pair, 0)

    @pl.when(n % 2 == 1)
    def _():
        update_from(sa_ref, hi - 1, last=True)

    @pl.when(n % 2 == 0)
    def _():
        logits_into(sb_ref, hi - 1)
        update_from(sa_ref, hi - 2)
        update_from(sb_ref, hi - 1, last=True)


def _flash_t_scratch(tk, n_queries, n_value_rows):
    return [pltpu.VMEM((tk, n_queries), F32), pltpu.VMEM((tk, n_queries), F32),
            pltpu.VMEM((1, n_queries), F32), pltpu.VMEM((n_value_rows, n_queries), F32)]


def _dilated_kernel(t, reach, qT_ref, k_ref, vT_ref, bias_ref, o_ref, *scratch):
    i = pl.program_id(2)
    q0 = i * t
    qT_s = _split_heads_t(qT_ref[0])

    def bias_of(kb, last):
        return bias_ref[(q0 - kb * t) // t]

    _skewed_flash_t(jnp.maximum(q0 - reach, 0) // t, i + 1, t, qT_s, k_ref, vT_ref, bias_of, *scratch)
    acc = scratch[3][...]
    w = HEAD_DIM + VT_PAD
    o = jnp.concatenate(
        [acc[0:HEAD_DIM, 0:t] * (1.0 / acc[HEAD_DIM:HEAD_DIM + 1, 0:t]),
         acc[w:w + HEAD_DIM, t:2 * t] * (1.0 / acc[w + HEAD_DIM:w + HEAD_DIM + 1, t:2 * t])], axis=0)
    o_ref[...] = o.T


def _dilated_attention(qT, k, vT, bsz, s_len, t):
    t_tokens = k.shape[0]
    nq = s_len // t
    reach = max(wd for wd, _ in C_PATTERNS)
    bias = jnp.swapaxes(_bias_tables((reach + t - 1) // t + 1, t, t, t, _dilated_log2_multiplicity),
                        1, 2)
    return pl.pallas_call(
        functools.partial(_dilated_kernel, t, reach),
        grid=(bsz, N_PAIRS, nq),
        in_specs=[
            pl.BlockSpec((1, LANES, t), lambda b, j, i: (j, 0, b * nq + i)),
            pl.BlockSpec((s_len, LANES), lambda b, j, i: (b, j)),
            pl.BlockSpec((1, vT.shape[1], s_len), lambda b, j, i: (j, 0, b)),
            pl.BlockSpec(bias.shape, lambda b, j, i: (0, 0, 0)),
        ],
        out_specs=pl.BlockSpec((t, LANES), lambda b, j, i: (b * nq + i, j)),
        out_shape=jax.ShapeDtypeStruct((t_tokens, GROUP), F32),
        scratch_shapes=_flash_t_scratch(t, 2 * t, vT.shape[1]),
        compiler_params=_params("parallel", "parallel", "parallel"),
        name="dilated_attention",
    )(qT, k, vT, bias)


def _diff_kernel(t, lambda_init, qT_ref, k_ref, vT_ref, bias_ref, lq1_ref, lk1_ref, lq2_ref,
                 lk2_ref, sg_ref, o_ref, *scratch):
    i = pl.program_id(2)
    qT_s = _split_heads_t(qT_ref[0])

    def bias_of(kb, last):
        return bias_ref[...] if last else None

    _skewed_flash_t(0, i + 1, t, qT_s, k_ref, vT_ref, bias_of, *scratch)
    acc = scratch[3][...]
    o = acc[0:LANES] * (1.0 / acc[LANES:LANES + 1])
    lam = (jnp.exp(jnp.sum(lq1_ref[...] * lk1_ref[...], axis=1, keepdims=True))
           - jnp.exp(jnp.sum(lq2_ref[...] * lk2_ref[...], axis=1, keepdims=True)) + lambda_init)
    o = o[:, 0:t] - lam * o[:, t:2 * t]
    ms = jnp.mean(o * o, axis=0, keepdims=True)
    o_ref[...] = ((o * lax.rsqrt(ms + NORM_EPS) * sg_ref[...]) * (1.0 - lambda_init)).T


def _diff_attention(qT, k, vT, lq1, lk1, lq2, lk2, sub_gain, lambda_init, bsz, s_len, t):
    t_tokens = k.shape[0]
    nq = s_len // t
    bias = _bias_tables(1, t, t, t, lambda d: np.where(d >= 0, 0.0, NEG))[0].T
    vec = lambda a: a.reshape(1, -1).astype(F32)
    small = lambda n: pl.BlockSpec((1, n), lambda b, j, i: (0, 0))
    return pl.pallas_call(
        functools.partial(_diff_kernel, t, lambda_init),
        grid=(bsz, N_PAIRS, nq),
        in_specs=[
            pl.BlockSpec((1, LANES, t), lambda b, j, i: (j, 0, b * nq + i)),
            pl.BlockSpec((s_len, LANES), lambda b, j, i: (b, j)),
            pl.BlockSpec((1, vT.shape[1], s_len), lambda b, j, i: (j, 0, b)),
            pl.BlockSpec((t, t), lambda b, j, i: (0, 0)),
            small(HEAD_DIM), small(HEAD_DIM), small(HEAD_DIM), small(HEAD_DIM),
            pl.BlockSpec((LANES, 1), lambda b, j, i: (0, 0)),
        ],
        out_specs=pl.BlockSpec((t, LANES), lambda b, j, i: (b * nq + i, j)),
        out_shape=jax.ShapeDtypeStruct((t_tokens, GROUP), F32),
        scratch_shapes=_flash_t_scratch(t, 2 * t, vT.shape[1]),
        compiler_params=_params("parallel", "parallel", "parallel"),
        name="diff_attention",
    )(qT, k, vT, bias, vec(lq1), vec(lk1), vec(lq2), vec(lk2), sub_gain.reshape(-1, 1).astype(F32))


def _outproj_kernel(o1_ref, o2_ref, g_ref, x_ref, p_ref, wo_ref, pg_ref, wg_ref, wp_ref, out_ref):
    g = g_ref[...]
    y1 = (o1_ref[...] * g[:, 0:GROUP]).astype(BF16)
    y2 = (o2_ref[...] * g[:, GROUP:2 * GROUP]).astype(BF16)
    x1 = (x_ref[...]
          + jnp.dot(y1, wo_ref[0:GROUP, :], preferred_element_type=F32)
          + jnp.dot(y2, wo_ref[GROUP:2 * GROUP, :], preferred_element_type=F32))
    ms = jnp.mean(x1 * x1, axis=-1, keepdims=True)
    hn = (x1 * lax.rsqrt(ms + NORM_EPS) * pg_ref[...]).astype(BF16)
    z = jnp.dot(hn, wg_ref[...], preferred_element_type=F32)
    gate = 1.0 / (1.0 + jnp.exp(-z))
    pp = jnp.dot(p_ref[...].astype(BF16), wp_ref[...], preferred_element_type=F32)
    out_ref[...] = x1 + pp * gate


def _outproj(o1, o2, gate, x2, p2, w_out, ple_gain, w_gate, w_proj, tm):
    t_tokens = x2.shape[0]
    tok = lambda n: pl.BlockSpec((tm, n), lambda i: (i, 0))
    full = lambda a: pl.BlockSpec(a.shape, lambda i: (0, 0))
    pg = ple_gain.reshape(1, D_MODEL)
    return pl.pallas_call(
        _outproj_kernel,
        grid=(t_tokens // tm,),
        in_specs=[tok(GROUP), tok(GROUP), tok(2 * GROUP), tok(D_MODEL), tok(PLE_DIM),
                  full(w_out), full(pg), full(w_gate), full(w_proj)],
        out_specs=tok(D_MODEL),
        out_shape=jax.ShapeDtypeStruct((t_tokens, D_MODEL), F32),
        compiler_params=_params("parallel"),
        name="outproj_ple",
    )(o1, o2, gate, x2, p2, w_out, pg, w_gate, w_proj)


TM = 256
TQ_DSA = 128
TK_DSA = 512
TQ_WINDOW = 256
T_DILATED = 512
T_DIFF = 512


def _rope_tables(s_len):
    half = HEAD_DIM // 2
    inv = ROPE_THETA ** (-jnp.arange(half, dtype=F32) / half)
    ang = jnp.arange(s_len).astype(F32)[:, None] * inv[None, :]
    cos = jnp.tile(jnp.cos(ang), (1, 4))
    sin = jnp.sin(ang)
    sin_signed = jnp.tile(jnp.concatenate([-sin, sin], axis=1), (1, 2))
    return cos, sin_signed


def _pair_gain(g):
    return jnp.concatenate([g, g]).astype(F32)


def _even_layer(x2, p2, bsz, s_len, tables, norm_gain, w_in, w_out, a_q_gain, a_k_gain, idx_k_gain,
                b_q_gain, b_k_gain, b_sinks, ple_gain, w_gate, w_proj):
    hd = HEAD_DIM
    offs = np.cumsum([0, 512, hd, hd, 512, hd, 8, 512, 512, 2 * hd, 2 * hd, 512])
    aq, ak, av, iq, ik, iw, ag, bq, bk, bv, bg = [w_in[:, offs[n]:offs[n + 1]] for n in range(11)]
    dup = lambda c: jnp.concatenate([c, c], axis=1)
    w = jnp.concatenate(
        [aq, iq, bq, ag, bg,
         dup(ak), dup(ik), dup(bk[:, :hd]), dup(bk[:, hd:]),
         dup(av), dup(bv[:, :hd]), dup(bv[:, hd:]),
         jnp.pad(iw, ((0, 0), (0, LANES - 8)))], axis=1).astype(BF16)
    ones = jnp.ones((LANES,), F32)
    gains = jnp.stack(
        [_pair_gain(a_q_gain)] * 4 + [ones] * 4 + [_pair_gain(b_q_gain)] * 4 + [ones] * 8
        + [_pair_gain(a_k_gain), _pair_gain(idx_k_gain), _pair_gain(b_k_gain), _pair_gain(b_k_gain)]
        + [ones] * 4)
    idx_scale = (8 * hd) ** -0.5
    plan = ([("k", 0, r, QSCALE) for r in range(4)]
            + [("q", 1, c, False, 1.0) for c in range(4)]
            + [("q", 2, c, True, QSCALE) for c in range(4)]
            + [("silu", 3, c) for c in range(8)]
            + [("q", 4, 0, True, 1.0)]
            + [("k", 5, r, 1.0) for r in range(3)]
            + [("vt", 6, 0, HEAD_DIM, 1)]
            + [("v2", 7, 2 * r) for r in range(2)]
            + [("iw", 8, idx_scale)])
    out_defs = [("kt", 4, BF16), ("tok", 512, BF16), ("tok", 512, BF16), ("tok", 1024, F32),
                ("tok", LANES, BF16), ("kt", 3, BF16), ("vt", (1, HEAD_DIM + VT_PAD), BF16),
                ("v", 4, BF16), ("tok", LANES, F32)]
    qaT, qi, qb, gate, ka, kT, vaT, vb, iwt = _inproj(x2, norm_gain, w, gains, *tables, plan,
                                                      out_defs, TM)
    oa = _dsa_attention(qi, iwt, kT, qaT, ka, vaT, bsz, s_len, min(TQ_DSA, s_len), TK_DSA)
    ob = _window_attention(qb, kT, vb, b_sinks.astype(F32), bsz, s_len, TQ_WINDOW)
    return _outproj(oa, ob, gate, x2, p2, w_out.astype(BF16), ple_gain, w_gate.astype(BF16),
                    w_proj.astype(BF16), TM)


def _odd_layer(x2, p2, bsz, s_len, tables, norm_gain, w_in, w_out, c_q_gain, c_k_gain, d_q_gain,
               d_k_gain, lq1, lk1, lq2, lk2, sub_gain, lambda_init, ple_gain, w_gate, w_proj):
    ones = jnp.ones((LANES,), F32)
    gains = jnp.stack([_pair_gain(c_q_gain)] * 4 + [_pair_gain(c_k_gain)] * 4 + [ones] * 8
                      + [_pair_gain(d_q_gain)] * 4 + [_pair_gain(d_k_gain)] * 4 + [ones] * 8)
    plan = ([("k", 0, r, QSCALE) for r in range(4)]
            + [("q", 1, c, True, 1.0) for c in range(4)]
            + [("vt", 2, r, HEAD_DIM, 2) for r in range(4)]
            + [("silu", 3, c) for c in range(4)]
            + [("k", 4, r, QSCALE) for r in range(4)]
            + [("q", 5, c, True, 1.0) for c in range(4)]
            + [("vt", 6, r, 2 * HEAD_DIM, 1) for r in range(4)]
            + [("silu", 3, 4 + c) for c in range(4)])
    out_defs = [("kt", 4, BF16), ("tok", 512, BF16), ("vt", (4, 2 * (HEAD_DIM + VT_PAD)), BF16),
                ("tok", 1024, F32),
                ("kt", 4, BF16), ("tok", 512, BF16), ("vt", (4, 2 * HEAD_DIM + VT_PAD), BF16)]
    qcT, kc, vcT, gate, qdT, kd, vdT = _inproj(x2, norm_gain, w_in.astype(BF16), gains, *tables,
                                               plan, out_defs, TM)
    oc = _dilated_attention(qcT, kc, vcT, bsz, s_len, T_DILATED)
    od = _diff_attention(qdT, kd, vdT, lq1, lk1, lq2, lk2, sub_gain, lambda_init, bsz, s_len, T_DIFF)
    return _outproj(oc, od, gate, x2, p2, w_out.astype(BF16), ple_gain, w_gate.astype(BF16),
                    w_proj.astype(BF16), TM)


def kernel(x, p, norm_gain, w_in_even, w_out_even, a_q_gain, a_k_gain, idx_k_gain, b_q_gain, b_k_gain, b_sinks, w_in_odd, w_out_odd, c_q_gain, c_k_gain, d_q_gain, d_k_gain, d_lambda_q1, d_lambda_k1, d_lambda_q2, d_lambda_k2, d_subln_gain, ple_norm_gain, w_ple_gate, w_ple_proj):
    bsz, s_len, d_model = x.shape
    depth = p.shape[0]
    assert d_model == D_MODEL and s_len % 512 == 0
    tables = _rope_tables(s_len)
    x2 = x.reshape(bsz * s_len, d_model)
    for i in range(depth):
        j = i // 2
        p2 = p[i].reshape(bsz * s_len, PLE_DIM)
        if i % 2 == 0:
            x2 = _even_layer(x2, p2, bsz, s_len, tables, norm_gain[i], w_in_even[j], w_out_even[j],
                             a_q_gain[j], a_k_gain[j], idx_k_gain[j], b_q_gain[j], b_k_gain[j],
                             b_sinks[j], ple_norm_gain[i], w_ple_gate[i], w_ple_proj[i])
        else:
            lambda_init = 0.8 - 0.6 * math.exp(-0.3 * i)
            x2 = _odd_layer(x2, p2, bsz, s_len, tables, norm_gain[i], w_in_odd[j], w_out_odd[j],
                            c_q_gain[j], c_k_gain[j], d_q_gain[j], d_k_gain[j], d_lambda_q1[j],
                            d_lambda_k1[j], d_lambda_q2[j], d_lambda_k2[j], d_subln_gain[j],
                            lambda_init, ple_norm_gain[i], w_ple_gate[i], w_ple_proj[i])
    return x2.reshape(bsz, s_len, d_model)
```

```python
import functools
import math

import numpy as np
import jax
import jax.numpy as jnp
from jax import lax
from jax.experimental import pallas as pl
from jax.experimental.pallas import tpu as pltpu

F32 = jnp.float32
BF16 = jnp.bfloat16
I32 = jnp.int32

D_MODEL = 1024
HEAD_DIM = 64
ROPE_THETA = 10000.0
NORM_EPS = 1e-6
PLE_DIM = 256
TOPK_MAX = 256
B_WINDOW = 128
C_PATTERNS = ((128, 1), (512, 4), (2048, 16))
N_PAIRS = 4
LANES = 128
GROUP = 512
VT_PAD = 16
LOG2E = 1.4426950408889634
QSCALE = HEAD_DIM ** -0.5 * LOG2E
NEG = -1e30
F32_MAX = float(np.finfo(np.float32).max)
FIXED_PROBES = 16
MAX_PROBES = 20
VMEM_LIMIT = 48 * 1024 * 1024


def _params(*sem):
    return pltpu.CompilerParams(dimension_semantics=sem, vmem_limit_bytes=VMEM_LIMIT)


def _norm_rope(y, gain, cos, sin_signed, bd, first_half, norm):
    if norm:
        ss = y * y
        hi = ss.astype(BF16)
        lo = (ss - hi.astype(F32)).astype(BF16)
        seg = (jnp.dot(hi, bd, preferred_element_type=F32)
               + jnp.dot(lo, bd, preferred_element_type=F32))
        y = y * lax.rsqrt(seg * (1.0 / HEAD_DIM) + NORM_EPS) * gain
    rot = jnp.where(first_half, pltpu.roll(y, 96, 1), pltpu.roll(y, 32, 1))
    return y * cos + rot * sin_signed


def _inproj_kernel(plan, x_ref, ng_ref, w_ref, cos_ref, sin_ref, gains_ref, bd_ref, *outs):
    x = x_ref[...]
    ms = jnp.mean(x * x, axis=-1, keepdims=True)
    h = (x * lax.rsqrt(ms + NORM_EPS) * ng_ref[...]).astype(BF16)
    cos = cos_ref[...]
    sin = sin_ref[...]
    bd = bd_ref[...]
    tm = x.shape[0]
    lane = lax.broadcasted_iota(I32, (tm, LANES), 1)
    lo_half = lane < HEAD_DIM
    first_half = (lane & (HEAD_DIM - 1)) < HEAD_DIM // 2
    one_at_64 = jnp.where(lane == HEAD_DIM, 1.0, 0.0)
    one_at_0 = jnp.where(lane == 0, 1.0, 0.0)
    for g in range(len(plan) // 4):
        yg = jnp.dot(h, w_ref[:, g * GROUP:(g + 1) * GROUP], preferred_element_type=F32)
        for c4 in range(4):
            c = g * 4 + c4
            y = yg[:, c4 * LANES:(c4 + 1) * LANES]
            step = plan[c]
            kind, out = step[0], outs[step[1]]
            if kind == "q":
                _, _, col, norm, mult = step
                y = _norm_rope(y, gains_ref[c:c + 1, :], cos, sin, bd, first_half, norm)
                out[:, col * LANES:(col + 1) * LANES] = (y * mult).astype(BF16)
            elif kind == "k":
                _, _, row, mult = step
                y = _norm_rope(y, gains_ref[c:c + 1, :], cos, sin, bd, first_half, True)
                out[row] = (y * mult).T.astype(BF16)
            elif kind == "vt":
                _, _, row, head_dim, n_heads = step
                yT = y.T.astype(BF16)
                sub = lax.broadcasted_iota(I32, (VT_PAD, tm), 0)
                ones_row = jnp.where(sub == 0, 1.0, 0.0).astype(BF16)
                for n in range(n_heads):
                    base = n * (head_dim + VT_PAD)
                    out[row, base:base + head_dim, :] = yT[n * head_dim:(n + 1) * head_dim]
                    out[row, base + head_dim:base + head_dim + VT_PAD, :] = ones_row
            elif kind == "silu":
                _, _, col = step
                out[:, col * LANES:(col + 1) * LANES] = y * (1.0 / (1.0 + jnp.exp(-y)))
            elif kind == "v2":
                _, _, row = step
                out[row] = jnp.where(lo_half, y, one_at_64).astype(BF16)
                out[row + 1] = jnp.where(lo_half, one_at_0, y).astype(BF16)
            elif kind == "v1":
                _, _, row = step
                out[row, :, 0:LANES] = y.astype(BF16)
                out[row, :, LANES:2 * LANES] = one_at_0.astype(BF16)
            elif kind == "iw":
                _, _, mult = step
                out[...] = y * mult
            else:
                raise ValueError(kind)


def _inproj(x2, norm_gain, w, gains, cos_t, sin_t, plan, out_defs, tm):
    t_tokens = x2.shape[0]
    s_len = cos_t.shape[0]
    n_cols = w.shape[1]
    n_tiles = t_tokens // tm
    s_tiles = s_len // tm
    bd = jnp.asarray(np.kron(np.eye(2), np.ones((HEAD_DIM, HEAD_DIM))), BF16)
    out_shapes, out_specs = [], []
    for kind, n, dtype in out_defs:
        if kind == "tok":
            out_shapes.append(jax.ShapeDtypeStruct((t_tokens, n), dtype))
            out_specs.append(pl.BlockSpec((tm, n), lambda i: (i, 0)))
        elif kind == "kt":
            out_shapes.append(jax.ShapeDtypeStruct((n, LANES, t_tokens), dtype))
            out_specs.append(pl.BlockSpec((n, LANES, tm), lambda i: (0, 0, i)))
        elif kind == "v":
            out_shapes.append(jax.ShapeDtypeStruct((n, t_tokens, LANES), dtype))
            out_specs.append(pl.BlockSpec((n, tm, LANES), lambda i: (0, i, 0)))
        elif kind == "vt":
            out_shapes.append(jax.ShapeDtypeStruct((n[0], n[1], t_tokens), dtype))
            out_specs.append(pl.BlockSpec((n[0], n[1], tm), lambda i: (0, 0, i)))
        elif kind == "vw":
            out_shapes.append(jax.ShapeDtypeStruct((n, t_tokens, 2 * LANES), dtype))
            out_specs.append(pl.BlockSpec((n, tm, 2 * LANES), lambda i: (0, i, 0)))
    return pl.pallas_call(
        functools.partial(_inproj_kernel, plan),
        grid=(n_tiles,),
        in_specs=[
            pl.BlockSpec((tm, D_MODEL), lambda i: (i, 0)),
            pl.BlockSpec((1, D_MODEL), lambda i: (0, 0)),
            pl.BlockSpec((D_MODEL, n_cols), lambda i: (0, 0)),
            pl.BlockSpec((tm, LANES), lambda i: (i % s_tiles, 0)),
            pl.BlockSpec((tm, LANES), lambda i: (i % s_tiles, 0)),
            pl.BlockSpec(gains.shape, lambda i: (0, 0)),
            pl.BlockSpec((LANES, LANES), lambda i: (0, 0)),
        ],
        out_specs=out_specs,
        out_shape=out_shapes,
        compiler_params=_params("parallel"),
        name="inproj",
    )(x2, norm_gain.reshape(1, D_MODEL), w, cos_t, sin_t, gains, bd)


def _split_heads(qc):
    lane = lax.broadcasted_iota(I32, qc.shape, 1)
    zero = jnp.zeros_like(qc)
    return jnp.where(lane < HEAD_DIM, qc, zero), jnp.where(lane >= HEAD_DIM, qc, zero)


def _stack_heads(q_ref):
    tq, width = q_ref.shape
    parts = []
    for j in range(width // LANES):
        parts.extend(_split_heads(q_ref[:, j * LANES:(j + 1) * LANES]))
    return jnp.concatenate(parts, axis=0)


def _merge_heads(acc_lo, acc_hi):
    lane = lax.broadcasted_iota(I32, acc_lo.shape, 1)
    o_lo = acc_lo * (1.0 / acc_lo[:, HEAD_DIM:HEAD_DIM + 1])
    o_hi = acc_hi * (1.0 / acc_hi[:, 0:1])
    return jnp.where(lane < HEAD_DIM, o_lo, o_hi)


def _dsa_kernel(tq, tk, topk, s_len, qi_ref, iw_ref, kiT_ref, qaT_ref, ka_ref, vaT_ref, o_ref,
                sc_ref, tie_ref, *flash_scratch):
    i = pl.program_id(1)
    q0 = i * tq
    nkb = (q0 + tq + tk - 1) // tk
    n_sub = tk // LANES
    n_heads = 2 * N_PAIRS
    row = q0 + lax.broadcasted_iota(I32, (tq, tk), 0)
    col_iota = lax.broadcasted_iota(I32, (tq, tk), 1)

    w = iw_ref[...]
    wb = [jnp.broadcast_to(w[:, h:h + 1], (tq, LANES)) for h in range(n_heads)]
    q_idx = _stack_heads(qi_ref)

    def score_block(kb, carry):
        k0 = pl.multiple_of(kb * tk, tk)
        s = jnp.dot(q_idx, kiT_ref[0, :, pl.ds(k0, tk)], preferred_element_type=F32)
        accs = [jnp.zeros((tq, LANES), F32)] * n_sub
        for h in range(n_heads):
            accs = [accs[c] + wb[h] * jnp.maximum(
                        s[h * tq:(h + 1) * tq, c * LANES:(c + 1) * LANES], 0.0)
                    for c in range(n_sub)]
        score = jnp.concatenate(accs, axis=1)
        sc_ref[:, pl.ds(k0, tk)] = jnp.where(col_iota + k0 <= row, score, -jnp.inf)
        return carry

    lax.fori_loop(0, nkb, score_block, 0)

    def fold_rows(step, init, reduce):
        def blk(kb, a):
            k0 = pl.multiple_of(kb * tk, tk)
            x = sc_ref[:, pl.ds(k0, tk)]
            for c in range(n_sub):
                a = step(a, x[:, c * LANES:(c + 1) * LANES], k0 + c * LANES)
            return a
        a = lax.fori_loop(0, nkb, blk, jnp.full((tq, LANES), init, F32))
        return reduce(a, axis=1, keepdims=True)

    def count(pred):
        return fold_rows(lambda a, x, c0: a + jnp.where(pred(x, c0), 1.0, 0.0), 0.0, jnp.sum)

    keep_all = (row[:, 0:1] + 1).astype(F32) <= topk
    row_max = fold_rows(lambda a, x, c0: jnp.maximum(a, x), -jnp.inf, jnp.max)
    row_min = fold_rows(lambda a, x, c0: jnp.minimum(a, jnp.where(x == -jnp.inf, jnp.inf, x)),
                        jnp.inf, jnp.min)

    RUN, DONE = 0.0, 1.0

    def probe(_, carry):
        lo, hi, state, thr = carry
        cand = lo + (hi - lo) * 0.5
        collapsed = (cand <= lo) | (cand >= hi)
        tot = count(lambda x, c0: x >= cand)
        running = state == RUN
        hit = running & (tot == topk)
        state = jnp.where(hit, DONE, jnp.where(running & collapsed, 2.0, state))
        thr = jnp.where(hit, cand, thr)
        return jnp.where(tot > topk, cand, lo), jnp.where(tot < topk, cand, hi), state, thr

    def n_running(state):
        return jnp.sum(jnp.where(state == RUN, 1.0, 0.0))

    def more_probes(carry):
        _, it, rest = carry
        rest = probe(0, probe(0, rest))
        return jnp.where(it + 2 >= MAX_PROBES, 0.0, n_running(rest[2])), it + 2, rest

    state0 = jnp.where(keep_all, DONE, RUN)
    hi0 = row_max + (jnp.abs(row_max) * 2.0 ** -20 + 1e-30)
    rest = lax.fori_loop(0, FIXED_PROBES, probe,
                         (row_min, hi0, state0, jnp.full((tq, 1), -F32_MAX, F32)))
    _, _, (_, hi, state, thr) = lax.while_loop(
        lambda c: c[0] > 0.0, more_probes, (n_running(rest[2]), jnp.int32(FIXED_PROBES), rest))
    n_unresolved = jnp.sum(jnp.where(state == DONE, 0.0, 1.0))

    @pl.when(n_unresolved == 0.0)
    def _():
        def bias_block(kb, carry):
            k0 = pl.multiple_of(kb * tk, tk)
            sc_ref[:, pl.ds(k0, tk)] = jnp.where(sc_ref[:, pl.ds(k0, tk)] >= thr, 0.0, NEG)
            return carry
        lax.fori_loop(0, nkb, bias_block, 0)

    @pl.when(n_unresolved > 0.0)
    def _():
        def step_down(carry):
            _, hi, kth, found = carry
            below = fold_rows(lambda a, x, c0: jnp.maximum(a, jnp.where(x < hi, x, -jnp.inf)),
                              -jnp.inf, jnp.max)
            tot = count(lambda x, c0: x >= below)
            newly = (found == 0.0) & (tot >= topk)
            kth = jnp.where(newly, below, kth)
            found = jnp.where(newly, 1.0, found)
            return jnp.sum(1.0 - found), jnp.where(found == 0.0, below, hi), kth, found

        found0 = jnp.where(state == DONE, 1.0, 0.0)
        _, _, kth, _ = lax.while_loop(lambda c: c[0] > 0.0, step_down,
                                      (jnp.sum(1.0 - found0), hi, thr, found0))
        need = topk - count(lambda x, c0: x > kth)
        n_tie = count(lambda x, c0: x == kth)
        tie_ref[...] = jnp.full((tq, 1), s_len, I32)

        @pl.when(jnp.sum(jnp.where((state != DONE) & (n_tie > need), 1.0, 0.0)) > 0.0)
        def _():
            n_bits = max(1, int(math.ceil(math.log2(s_len))))
            lane_iota = lax.broadcasted_iota(I32, (tq, LANES), 1)

            def tie_step(j, p):
                cand = p + lax.shift_left(jnp.int32(1), n_bits - 1 - j)
                tot = count(lambda x, c0: (x == kth) & (lane_iota + c0 < cand))
                return jnp.where(tot < need, cand, p)

            tie_col = lax.fori_loop(0, n_bits, tie_step, jnp.zeros((tq, 1), I32))
            tie_ref[...] = jnp.where(state == DONE, s_len, tie_col)

        tie_col = tie_ref[...]

        def bias_block(kb, carry):
            k0 = pl.multiple_of(kb * tk, tk)
            x = sc_ref[:, pl.ds(k0, tk)]
            tie_bias = jnp.where(col_iota + k0 <= tie_col, 0.0, NEG)
            sc_ref[:, pl.ds(k0, tk)] = jnp.where(x > kth, 0.0, jnp.where(x == kth, tie_bias, NEG))
            return carry
        lax.fori_loop(0, nkb, bias_block, 0)

    qT_s = jnp.concatenate([_split_heads_t(qaT_ref[j]) for j in range(N_PAIRS)], axis=1)

    def bias_of(kb, last):
        return sc_ref[:, pl.ds(pl.multiple_of(kb * tk, tk), tk)].T

    _skewed_flash_t(0, nkb, tk, qT_s, ka_ref, vaT_ref, bias_of, *flash_scratch)
    acc = flash_scratch[3][...]
    o = acc[0:HEAD_DIM] * (1.0 / acc[HEAD_DIM:HEAD_DIM + 1])
    for j in range(N_PAIRS):
        pair = jnp.concatenate([o[:, 2 * j * tq:(2 * j + 1) * tq],
                                o[:, (2 * j + 1) * tq:(2 * j + 2) * tq]], axis=0)
        o_ref[:, j * LANES:(j + 1) * LANES] = pair.T


def _dsa_attention(qi, iw, kiT, qaT, ka, vaT, bsz, s_len, tq, tk):
    t_tokens = qi.shape[0]
    nq = s_len // tq
    topk = min(TOPK_MAX, s_len // 4)
    return pl.pallas_call(
        functools.partial(_dsa_kernel, tq, tk, float(topk), s_len),
        grid=(bsz, nq),
        in_specs=[
            pl.BlockSpec((tq, GROUP), lambda b, i: (b * nq + i, 0)),
            pl.BlockSpec((tq, LANES), lambda b, i: (b * nq + i, 0)),
            pl.BlockSpec((1, LANES, s_len), lambda b, i: (0, 0, b)),
            pl.BlockSpec((N_PAIRS, LANES, tq), lambda b, i: (0, 0, b * nq + i)),
            pl.BlockSpec((s_len, LANES), lambda b, i: (b, 0)),
            pl.BlockSpec((1, vaT.shape[1], s_len), lambda b, i: (0, 0, b)),
        ],
        out_specs=pl.BlockSpec((tq, GROUP), lambda b, i: (b * nq + i, 0)),
        out_shape=jax.ShapeDtypeStruct((t_tokens, GROUP), F32),
        scratch_shapes=[pltpu.VMEM((tq, s_len), F32), pltpu.VMEM((tq, 1), I32)]
        + _flash_t_scratch(tk, 2 * N_PAIRS * tq, vaT.shape[1]),
        compiler_params=_params("parallel", "parallel"),
        name="dsa_attention",
    )(qi, iw, kiT, qaT, ka, vaT)


def _bias_tables(n, step, tq, tk, fn):
    d = np.arange(n)[:, None, None] * step + np.arange(tq)[None, :, None] - np.arange(tk)[None, None, :]
    return jnp.asarray(fn(d), F32)


def _window_kernel(tq, q_ref, kT_ref, v_ref, bias_ref, sink_ref, o_ref):
    j = pl.program_id(1)
    i = pl.program_id(2)
    q0 = i * tq
    wk = tq + B_WINDOW
    k0 = pl.multiple_of(jnp.maximum(q0 - B_WINDOW, 0), LANES)
    kT = kT_ref[0, :, pl.ds(k0, wk)]
    bias = bias_ref[(q0 - k0) // B_WINDOW]
    lane = lax.broadcasted_iota(I32, (tq, LANES), 1)
    s_both = jnp.dot(_stack_heads(q_ref), kT, preferred_element_type=F32)
    accs = []
    for half in range(2):
        sink = sink_ref[2 * j + half] * LOG2E
        s = s_both[half * tq:(half + 1) * tq] + bias
        m = jnp.maximum(jnp.max(s, axis=1, keepdims=True), sink)
        p = jnp.exp2(s - m).astype(BF16)
        acc = jnp.dot(p, v_ref[half, pl.ds(k0, wk), :], preferred_element_type=F32)
        accs.append(acc + jnp.where(lane == (HEAD_DIM if half == 0 else 0), jnp.exp2(sink - m), 0.0))
    o_ref[...] = _merge_heads(*accs)


def _window_attention(q, kT, v, sinks, bsz, s_len, tq):
    t_tokens = q.shape[0]
    nq = s_len // tq
    bias = _bias_tables(2, B_WINDOW, tq, tq + B_WINDOW,
                        lambda d: np.where((d >= 0) & (d < B_WINDOW), 0.0, NEG))
    return pl.pallas_call(
        functools.partial(_window_kernel, tq),
        grid=(bsz, N_PAIRS, nq),
        in_specs=[
            pl.BlockSpec((tq, LANES), lambda b, j, i: (b * nq + i, j)),
            pl.BlockSpec((1, LANES, s_len), lambda b, j, i: (1 + j // 2, 0, b)),
            pl.BlockSpec((2, s_len, LANES), lambda b, j, i: (j // 2, b, 0)),
            pl.BlockSpec(bias.shape, lambda b, j, i: (0, 0, 0)),
            pl.BlockSpec(memory_space=pltpu.SMEM),
        ],
        out_specs=pl.BlockSpec((tq, LANES), lambda b, j, i: (b * nq + i, j)),
        out_shape=jax.ShapeDtypeStruct((t_tokens, GROUP), F32),
        compiler_params=_params("parallel", "parallel", "parallel"),
        name="window_attention",
    )(q, kT, v, bias, sinks)


def _dilated_log2_multiplicity(d):
    count = np.zeros(d.shape)
    for window, dilation in C_PATTERNS:
        count += (d >= 0) & (d <= window) & (d % dilation == 0)
    return np.where(count > 0, np.log2(np.maximum(count, 1)), NEG)


def _split_heads_t(qT):
    row = lax.broadcasted_iota(I32, qT.shape, 0)
    zero = jnp.zeros_like(qT)
    return jnp.concatenate([jnp.where(row < HEAD_DIM, qT, zero), jnp.where(row >= HEAD_DIM, qT, zero)],
                           axis=1)


def _skewed_flash_t(lo, hi, tk, qT_s, k_ref, vT_ref, bias_of, sa_ref, sb_ref, m_ref, acc_ref):
    m_ref[...] = jnp.full(m_ref.shape, NEG, F32)
    acc_ref[...] = jnp.zeros(acc_ref.shape, F32)

    def logits_into(dst, kb):
        dst[...] = jnp.dot(k_ref[pl.ds(pl.multiple_of(kb * tk, tk), tk), :], qT_s,
                           preferred_element_type=F32)

    def update_from(src, kb, last=False):
        s = src[...]
        bias = bias_of(kb, last)
        if bias is not None:
            w = bias.shape[1]
            s = jnp.concatenate([s[:, r * w:(r + 1) * w] + bias for r in range(s.shape[1] // w)],
                                axis=1)
        m = m_ref[...]
        m_new = jnp.maximum(m, jnp.max(s, axis=0, keepdims=True))
        p = jnp.exp2(s - m_new).astype(BF16)
        vT = vT_ref[0, :, pl.ds(pl.multiple_of(kb * tk, tk), tk)]
        acc_ref[...] = jnp.exp2(m - m_new) * acc_ref[...] + jnp.dot(vT, p, preferred_element_type=F32)
        m_ref[...] = m_new

    n = hi - lo
    logits_into(sa_ref, lo)

    def pair(pp, carry):
        kb = lo + 2 * pp
        logits_into(sb_ref, kb + 1)
        update_from(sa_ref, kb)
        logits_into(sa_ref, kb + 2)
        update_from(sb_ref, kb + 1)
        return carry

    lax.fori_loop(0, (n - 1) // 2, pair, 0)

    @pl.when(n % 2 == 1)
    def _():
        update_from(sa_ref, hi - 1, last=True)

    @pl.when(n % 2 == 0)
    def _():
        logits_into(sb_ref, hi - 1)
        update_from(sa_ref, hi - 2)
        update_from(sb_ref, hi - 1, last=True)


def _flash_t_scratch(tk, n_queries, n_value_rows):
    return [pltpu.VMEM((tk, n_queries), F32), pltpu.VMEM((tk, n_queries), F32),
            pltpu.VMEM((1, n_queries), F32), pltpu.VMEM((n_value_rows, n_queries), F32)]


def _dilated_kernel(t, reach, qT_ref, k_ref, vT_ref, bias_ref, o_ref, *scratch):
    i = pl.program_id(2)
    q0 = i * t
    qT_s = _split_heads_t(qT_ref[0])

    def bias_of(kb, last):
        return bias_ref[(q0 - kb * t) // t]

    _skewed_flash_t(jnp.maximum(q0 - reach, 0) // t, i + 1, t, qT_s, k_ref, vT_ref, bias_of, *scratch)
    acc = scratch[3][...]
    w = HEAD_DIM + VT_PAD
    o = jnp.concatenate(
        [acc[0:HEAD_DIM, 0:t] * (1.0 / acc[HEAD_DIM:HEAD_DIM + 1, 0:t]),
         acc[w:w + HEAD_DIM, t:2 * t] * (1.0 / acc[w + HEAD_DIM:w + HEAD_DIM + 1, t:2 * t])], axis=0)
    o_ref[...] = o.T


def _dilated_attention(qT, k, vT, bsz, s_len, t):
    t_tokens = k.shape[0]
    nq = s_len // t
    reach = max(wd for wd, _ in C_PATTERNS)
    bias = jnp.swapaxes(_bias_tables((reach + t - 1) // t + 1, t, t, t, _dilated_log2_multiplicity),
                        1, 2)
    return pl.pallas_call(
        functools.partial(_dilated_kernel, t, reach),
        grid=(bsz, N_PAIRS, nq),
        in_specs=[
            pl.BlockSpec((1, LANES, t), lambda b, j, i: (j, 0, b * nq + i)),
            pl.BlockSpec((s_len, LANES), lambda b, j, i: (b, j)),
            pl.BlockSpec((1, vT.shape[1], s_len), lambda b, j, i: (j, 0, b)),
            pl.BlockSpec(bias.shape, lambda b, j, i: (0, 0, 0)),
        ],
        out_specs=pl.BlockSpec((t, LANES), lambda b, j, i: (b * nq + i, j)),
        out_shape=jax.ShapeDtypeStruct((t_tokens, GROUP), F32),
        scratch_shapes=_flash_t_scratch(t, 2 * t, vT.shape[1]),
        compiler_params=_params("parallel", "parallel", "parallel"),
        name="dilated_attention",
    )(qT, k, vT, bias)


def _diff_kernel(t, lambda_init, qT_ref, k_ref, vT_ref, bias_ref, lq1_ref, lk1_ref, lq2_ref,
                 lk2_ref, sg_ref, o_ref, *scratch):
    i = pl.program_id(2)
    qT_s = _split_heads_t(qT_ref[0])

    def bias_of(kb, last):
        return bias_ref[...] if last else None

    _skewed_flash_t(0, i + 1, t, qT_s, k_ref, vT_ref, bias_of, *scratch)
    acc = scratch[3][...]
    o = acc[0:LANES] * (1.0 / acc[LANES:LANES + 1])
    lam = (jnp.exp(jnp.sum(lq1_ref[...] * lk1_ref[...], axis=1, keepdims=True))
           - jnp.exp(jnp.sum(lq2_ref[...] * lk2_ref[...], axis=1, keepdims=True)) + lambda_init)
    o = o[:, 0:t] - lam * o[:, t:2 * t]
    ms = jnp.mean(o * o, axis=0, keepdims=True)
    o_ref[...] = ((o * lax.rsqrt(ms + NORM_EPS) * sg_ref[...]) * (1.0 - lambda_init)).T


def _diff_attention(qT, k, vT, lq1, lk1, lq2, lk2, sub_gain, lambda_init, bsz, s_len, t):
    t_tokens = k.shape[0]
    nq = s_len // t
    bias = _bias_tables(1, t, t, t, lambda d: np.where(d >= 0, 0.0, NEG))[0].T
    vec = lambda a: a.reshape(1, -1).astype(F32)
    small = lambda n: pl.BlockSpec((1, n), lambda b, j, i: (0, 0))
    return pl.pallas_call(
        functools.partial(_diff_kernel, t, lambda_init),
        grid=(bsz, N_PAIRS, nq),
        in_specs=[
            pl.BlockSpec((1, LANES, t), lambda b, j, i: (j, 0, b * nq + i)),
            pl.BlockSpec((s_len, LANES), lambda b, j, i: (b, j)),
            pl.BlockSpec((1, vT.shape[1], s_len), lambda b, j, i: (j, 0, b)),
            pl.BlockSpec((t, t), lambda b, j, i: (0, 0)),
            small(HEAD_DIM), small(HEAD_DIM), small(HEAD_DIM), small(HEAD_DIM),
            pl.BlockSpec((LANES, 1), lambda b, j, i: (0, 0)),
        ],
        out_specs=pl.BlockSpec((t, LANES), lambda b, j, i: (b * nq + i, j)),
        out_shape=jax.ShapeDtypeStruct((t_tokens, GROUP), F32),
        scratch_shapes=_flash_t_scratch(t, 2 * t, vT.shape[1]),
        compiler_params=_params("parallel", "parallel", "parallel"),
        name="diff_attention",
    )(qT, k, vT, bias, vec(lq1), vec(lk1), vec(lq2), vec(lk2), sub_gain.reshape(-1, 1).astype(F32))


def _outproj_kernel(o1_ref, o2_ref, g_ref, x_ref, p_ref, wo_ref, pg_ref, wg_ref, wp_ref, out_ref):
    g = g_ref[...]
    y1 = (o1_ref[...] * g[:, 0:GROUP]).astype(BF16)
    y2 = (o2_ref[...] * g[:, GROUP:2 * GROUP]).astype(BF16)
    x1 = (x_ref[...]
          + jnp.dot(y1, wo_ref[0:GROUP, :], preferred_element_type=F32)
          + jnp.dot(y2, wo_ref[GROUP:2 * GROUP, :], preferred_element_type=F32))
    ms = jnp.mean(x1 * x1, axis=-1, keepdims=True)
    hn = (x1 * lax.rsqrt(ms + NORM_EPS) * pg_ref[...]).astype(BF16)
    z = jnp.dot(hn, wg_ref[...], preferred_element_type=F32)
    gate = 1.0 / (1.0 + jnp.exp(-z))
    pp = jnp.dot(p_ref[...].astype(BF16), wp_ref[...], preferred_element_type=F32)
    out_ref[...] = x1 + pp * gate


def _outproj(o1, o2, gate, x2, p2, w_out, ple_gain, w_gate, w_proj, tm):
    t_tokens = x2.shape[0]
    tok = lambda n: pl.BlockSpec((tm, n), lambda i: (i, 0))
    full = lambda a: pl.BlockSpec(a.shape, lambda i: (0, 0))
    pg = ple_gain.reshape(1, D_MODEL)
    return pl.pallas_call(
        _outproj_kernel,
        grid=(t_tokens // tm,),
        in_specs=[tok(GROUP), tok(GROUP), tok(2 * GROUP), tok(D_MODEL), tok(PLE_DIM),
                  full(w_out), full(pg), full(w_gate), full(w_proj)],
        out_specs=tok(D_MODEL),
        out_shape=jax.ShapeDtypeStruct((t_tokens, D_MODEL), F32),
        compiler_params=_params("parallel"),
        name="outproj_ple",
    )(o1, o2, gate, x2, p2, w_out, pg, w_gate, w_proj)


TM = 256
TQ_DSA = 128
TK_DSA = 512
TQ_WINDOW = 256
T_DILATED = 512
T_DIFF = 512


def _rope_tables(s_len):
    half = HEAD_DIM // 2
    inv = ROPE_THETA ** (-jnp.arange(half, dtype=F32) / half)
    ang = jnp.arange(s_len).astype(F32)[:, None] * inv[None, :]
    cos = jnp.tile(jnp.cos(ang), (1, 4))
    sin = jnp.sin(ang)
    sin_signed = jnp.tile(jnp.concatenate([-sin, sin], axis=1), (1, 2))
    return cos, sin_signed


def _pair_gain(g):
    return jnp.concatenate([g, g]).astype(F32)


def _even_layer(x2, p2, bsz, s_len, tables, norm_gain, w_in, w_out, a_q_gain, a_k_gain, idx_k_gain,
                b_q_gain, b_k_gain, b_sinks, ple_gain, w_gate, w_proj):
    hd = HEAD_DIM
    offs = np.cumsum([0, 512, hd, hd, 512, hd, 8, 512, 512, 2 * hd, 2 * hd, 512])
    aq, ak, av, iq, ik, iw, ag, bq, bk, bv, bg = [w_in[:, offs[n]:offs[n + 1]] for n in range(11)]
    dup = lambda c: jnp.concatenate([c, c], axis=1)
    w = jnp.concatenate(
        [aq, iq, bq, ag, bg,
         dup(ak), dup(ik), dup(bk[:, :hd]), dup(bk[:, hd:]),
         dup(av), dup(bv[:, :hd]), dup(bv[:, hd:]),
         jnp.pad(iw, ((0, 0), (0, LANES - 8)))], axis=1).astype(BF16)
    ones = jnp.ones((LANES,), F32)
    gains = jnp.stack(
        [_pair_gain(a_q_gain)] * 4 + [ones] * 4 + [_pair_gain(b_q_gain)] * 4 + [ones] * 8
        + [_pair_gain(a_k_gain), _pair_gain(idx_k_gain), _pair_gain(b_k_gain), _pair_gain(b_k_gain)]
        + [ones] * 4)
    idx_scale = (8 * hd) ** -0.5
    plan = ([("k", 0, r, QSCALE) for r in range(4)]
            + [("q", 1, c, False, 1.0) for c in range(4)]
            + [("q", 2, c, True, QSCALE) for c in range(4)]
            + [("silu", 3, c) for c in range(8)]
            + [("q", 4, 0, True, 1.0)]
            + [("k", 5, r, 1.0) for r in range(3)]
            + [("vt", 6, 0, HEAD_DIM, 1)]
            + [("v2", 7, 2 * r) for r in range(2)]
            + [("iw", 8, idx_scale)])
    out_defs = [("kt", 4, BF16), ("tok", 512, BF16), ("tok", 512, BF16), ("tok", 1024, F32),
                ("tok", LANES, BF16), ("kt", 3, BF16), ("vt", (1, HEAD_DIM + VT_PAD), BF16),
                ("v", 4, BF16), ("tok", LANES, F32)]
    qaT, qi, qb, gate, ka, kT, vaT, vb, iwt = _inproj(x2, norm_gain, w, gains, *tables, plan,
                                                      out_defs, TM)
    oa = _dsa_attention(qi, iwt, kT, qaT, ka, vaT, bsz, s_len, min(TQ_DSA, s_len), TK_DSA)
    ob = _window_attention(qb, kT, vb, b_sinks.astype(F32), bsz, s_len, TQ_WINDOW)
    return _outproj(oa, ob, gate, x2, p2, w_out.astype(BF16), ple_gain, w_gate.astype(BF16),
                    w_proj.astype(BF16), TM)


def _odd_layer(x2, p2, bsz, s_len, tables, norm_gain, w_in, w_out, c_q_gain, c_k_gain, d_q_gain,
               d_k_gain, lq1, lk1, lq2, lk2, sub_gain, lambda_init, ple_gain, w_gate, w_proj):
    ones = jnp.ones((LANES,), F32)
    gains = jnp.stack([_pair_gain(c_q_gain)] * 4 + [_pair_gain(c_k_gain)] * 4 + [ones] * 8
                      + [_pair_gain(d_q_gain)] * 4 + [_pair_gain(d_k_gain)] * 4 + [ones] * 8)
    plan = ([("k", 0, r, QSCALE) for r in range(4)]
            + [("q", 1, c, True, 1.0) for c in range(4)]
            + [("vt", 2, r, HEAD_DIM, 2) for r in range(4)]
            + [("silu", 3, c) for c in range(4)]
            + [("k", 4, r, QSCALE) for r in range(4)]
            + [("q", 5, c, True, 1.0) for c in range(4)]
            + [("vt", 6, r, 2 * HEAD_DIM, 1) for r in range(4)]
            + [("silu", 3, 4 + c) for c in range(4)])
    out_defs = [("kt", 4, BF16), ("tok", 512, BF16), ("vt", (4, 2 * (HEAD_DIM + VT_PAD)), BF16),
                ("tok", 1024, F32),
                ("kt", 4, BF16), ("tok", 512, BF16), ("vt", (4, 2 * HEAD_DIM + VT_PAD), BF16)]
    qcT, kc, vcT, gate, qdT, kd, vdT = _inproj(x2, norm_gain, w_in.astype(BF16), gains, *tables,
                                               plan, out_defs, TM)
    oc = _dilated_attention(qcT, kc, vcT, bsz, s_len, T_DILATED)
    od = _diff_attention(qdT, kd, vdT, lq1, lk1, lq2, lk2, sub_gain, lambda_init, bsz, s_len, T_DIFF)
    return _outproj(oc, od, gate, x2, p2, w_out.astype(BF16), ple_gain, w_gate.astype(BF16),
                    w_proj.astype(BF16), TM)


def kernel(x, p, norm_gain, w_in_even, w_out_even, a_q_gain, a_k_gain, idx_k_gain, b_q_gain, b_k_gain, b_sinks, w_in_odd, w_out_odd, c_q_gain, c_k_gain, d_q_gain, d_k_gain, d_lambda_q1, d_lambda_k1, d_lambda_q2, d_lambda_k2, d_subln_gain, ple_norm_gain, w_ple_gate, w_ple_proj):
    bsz, s_len, d_model = x.shape
    depth = p.shape[0]
    assert d_model == D_MODEL and s_len % 512 == 0
    tables = _rope_tables(s_len)
    x2 = x.reshape(bsz * s_len, d_model)
    for i in range(depth):
        j = i // 2
        p2 = p[i].reshape(bsz * s_len, PLE_DIM)
        if i % 2 == 0:
            x2 = _even_layer(x2, p2, bsz, s_len, tables, norm_gain[i], w_in_even[j], w_out_even[j],
                             a_q_gain[j], a_k_gain[j], idx_k_gain[j], b_q_gain[j], b_k_gain[j],
                             b_sinks[j], ple_norm_gain[i], w_ple_gate[i], w_ple_proj[i])
        else:
            lambda_init = 0.8 - 0.6 * math.exp(-0.3 * i)
            x2 = _odd_layer(x2, p2, bsz, s_len, tables, norm_gain[i], w_in_odd[j], w_out_odd[j],
                            c_q_gain[j], c_k_gain[j], d_q_gain[j], d_k_gain[j], d_lambda_q1[j],
                            d_lambda_k1[j], d_lambda_q2[j], d_lambda_k2[j], d_subln_gain[j],
                            lambda_init, ple_norm_gain[i], w_ple_gate[i], w_ple_proj[i])
    return x2.reshape(bsz, s_len, d_model)
```

```python
import functools
import math

import numpy as np
import jax
import jax.numpy as jnp
from jax import lax
from jax.experimental import pallas as pl
from jax.experimental.pallas import tpu as pltpu

F32 = jnp.float32
BF16 = jnp.bfloat16
I32 = jnp.int32

D_MODEL = 1024
HEAD_DIM = 64
ROPE_THETA = 10000.0
NORM_EPS = 1e-6
PLE_DIM = 256
TOPK_MAX = 256
B_WINDOW = 128
C_PATTERNS = ((128, 1), (512, 4), (2048, 16))
N_PAIRS = 4
LANES = 128
GROUP = 512
VT_PAD = 16
LOG2E = 1.4426950408889634
QSCALE = HEAD_DIM ** -0.5 * LOG2E
NEG = -1e30
F32_MAX = float(np.finfo(np.float32).max)
FIXED_PROBES = 16
MAX_PROBES = 20
VMEM_LIMIT = 48 * 1024 * 1024


def _params(*sem):
    return pltpu.CompilerParams(dimension_semantics=sem, vmem_limit_bytes=VMEM_LIMIT)


def _norm_rope(y, gain, cos, sin_signed, bd, first_half, norm):
    if norm:
        ss = y * y
        hi = ss.astype(BF16)
        lo = (ss - hi.astype(F32)).astype(BF16)
        seg = (jnp.dot(hi, bd, preferred_element_type=F32)
               + jnp.dot(lo, bd, preferred_element_type=F32))
        y = y * lax.rsqrt(seg * (1.0 / HEAD_DIM) + NORM_EPS) * gain
    rot = jnp.where(first_half, pltpu.roll(y, 96, 1), pltpu.roll(y, 32, 1))
    return y * cos + rot * sin_signed


def _inproj_kernel(plan, x_ref, ng_ref, w_ref, cos_ref, sin_ref, gains_ref, bd_ref, *outs):
    x = x_ref[...]
    ms = jnp.mean(x * x, axis=-1, keepdims=True)
    h = (x * lax.rsqrt(ms + NORM_EPS) * ng_ref[...]).astype(BF16)
    cos = cos_ref[...]
    sin = sin_ref[...]
    bd = bd_ref[...]
    tm = x.shape[0]
    lane = lax.broadcasted_iota(I32, (tm, LANES), 1)
    lo_half = lane < HEAD_DIM
    first_half = (lane & (HEAD_DIM - 1)) < HEAD_DIM // 2
    one_at_64 = jnp.where(lane == HEAD_DIM, 1.0, 0.0)
    one_at_0 = jnp.where(lane == 0, 1.0, 0.0)
    for g in range(len(plan) // 4):
        yg = jnp.dot(h, w_ref[:, g * GROUP:(g + 1) * GROUP], preferred_element_type=F32)
        for c4 in range(4):
            c = g * 4 + c4
            y = yg[:, c4 * LANES:(c4 + 1) * LANES]
            step = plan[c]
            kind, out = step[0], outs[step[1]]
            if kind == "q":
                _, _, col, norm, mult = step
                y = _norm_rope(y, gains_ref[c:c + 1, :], cos, sin, bd, first_half, norm)
                out[:, col * LANES:(col + 1) * LANES] = (y * mult).astype(BF16)
            elif kind == "k":
                _, _, row, norm, mult = step
                y = _norm_rope(y, gains_ref[c:c + 1, :], cos, sin, bd, first_half, norm)
                out[row] = (y * mult).T.astype(BF16)
            elif kind == "vt":
                _, _, row, head_dim, n_heads = step
                yT = y.T.astype(BF16)
                sub = lax.broadcasted_iota(I32, (VT_PAD, tm), 0)
                ones_row = jnp.where(sub == 0, 1.0, 0.0).astype(BF16)
                for n in range(n_heads):
                    base = n * (head_dim + VT_PAD)
                    out[row, base:base + head_dim, :] = yT[n * head_dim:(n + 1) * head_dim]
                    out[row, base + head_dim:base + head_dim + VT_PAD, :] = ones_row
            elif kind == "silu":
                _, _, col = step
                out[:, col * LANES:(col + 1) * LANES] = y * (1.0 / (1.0 + jnp.exp(-y)))
            elif kind == "v2":
                _, _, row = step
                out[row] = jnp.where(lo_half, y, one_at_64).astype(BF16)
                out[row + 1] = jnp.where(lo_half, one_at_0, y).astype(BF16)
            elif kind == "v1":
                _, _, row = step
                out[row, :, 0:LANES] = y.astype(BF16)
                out[row, :, LANES:2 * LANES] = one_at_0.astype(BF16)
            elif kind == "iw":
                _, _, mult = step
                out[...] = (y * mult).T[0:8]
            else:
                raise ValueError(kind)


def _inproj(x2, norm_gain, w, gains, cos_t, sin_t, plan, out_defs, tm):
    t_tokens = x2.shape[0]
    s_len = cos_t.shape[0]
    n_cols = w.shape[1]
    n_tiles = t_tokens // tm
    s_tiles = s_len // tm
    bd = jnp.asarray(np.kron(np.eye(2), np.ones((HEAD_DIM, HEAD_DIM))), BF16)
    out_shapes, out_specs = [], []
    for kind, n, dtype in out_defs:
        if kind == "tok":
            out_shapes.append(jax.ShapeDtypeStruct((t_tokens, n), dtype))
            out_specs.append(pl.BlockSpec((tm, n), lambda i: (i, 0)))
        elif kind == "kt":
            out_shapes.append(jax.ShapeDtypeStruct((n, LANES, t_tokens), dtype))
            out_specs.append(pl.BlockSpec((n, LANES, tm), lambda i: (0, 0, i)))
        elif kind == "v":
            out_shapes.append(jax.ShapeDtypeStruct((n, t_tokens, LANES), dtype))
            out_specs.append(pl.BlockSpec((n, tm, LANES), lambda i: (0, i, 0)))
        elif kind == "vt":
            out_shapes.append(jax.ShapeDtypeStruct((n[0], n[1], t_tokens), dtype))
            out_specs.append(pl.BlockSpec((n[0], n[1], tm), lambda i: (0, 0, i)))
        elif kind == "rows":
            out_shapes.append(jax.ShapeDtypeStruct((n, t_tokens), dtype))
            out_specs.append(pl.BlockSpec((n, tm), lambda i: (0, i)))
        elif kind == "vw":
            out_shapes.append(jax.ShapeDtypeStruct((n, t_tokens, 2 * LANES), dtype))
            out_specs.append(pl.BlockSpec((n, tm, 2 * LANES), lambda i: (0, i, 0)))
    return pl.pallas_call(
        functools.partial(_inproj_kernel, plan),
        grid=(n_tiles,),
        in_specs=[
            pl.BlockSpec((tm, D_MODEL), lambda i: (i, 0)),
            pl.BlockSpec((1, D_MODEL), lambda i: (0, 0)),
            pl.BlockSpec((D_MODEL, n_cols), lambda i: (0, 0)),
            pl.BlockSpec((tm, LANES), lambda i: (i % s_tiles, 0)),
            pl.BlockSpec((tm, LANES), lambda i: (i % s_tiles, 0)),
            pl.BlockSpec(gains.shape, lambda i: (0, 0)),
            pl.BlockSpec((LANES, LANES), lambda i: (0, 0)),
        ],
        out_specs=out_specs,
        out_shape=out_shapes,
        compiler_params=_params("parallel"),
        name="inproj",
    )(x2, norm_gain.reshape(1, D_MODEL), w, cos_t, sin_t, gains, bd)


def _split_heads(qc):
    lane = lax.broadcasted_iota(I32, qc.shape, 1)
    zero = jnp.zeros_like(qc)
    return jnp.where(lane < HEAD_DIM, qc, zero), jnp.where(lane >= HEAD_DIM, qc, zero)


def _stack_heads(q_ref):
    tq, width = q_ref.shape
    parts = []
    for j in range(width // LANES):
        parts.extend(_split_heads(q_ref[:, j * LANES:(j + 1) * LANES]))
    return jnp.concatenate(parts, axis=0)


def _merge_heads(acc_lo, acc_hi):
    lane = lax.broadcasted_iota(I32, acc_lo.shape, 1)
    o_lo = acc_lo * (1.0 / acc_lo[:, HEAD_DIM:HEAD_DIM + 1])
    o_hi = acc_hi * (1.0 / acc_hi[:, 0:1])
    return jnp.where(lane < HEAD_DIM, o_lo, o_hi)


def _dsa_kernel(tq, tk, topk, s_len, qiT_ref, iwT_ref, ki_ref, qaT_ref, ka_ref, vaT_ref, o_ref,
                sc_ref, tie_ref, *flash_scratch):
    i = pl.program_id(1)
    q0 = i * tq
    nkb = (q0 + tq + tk - 1) // tk
    n_heads = 2 * N_PAIRS
    key_iota = lax.broadcasted_iota(I32, (tk, tq), 0)
    q_pos = q0 + lax.broadcasted_iota(I32, (tk, tq), 1)

    w = iwT_ref[...]
    qT_idx = jnp.concatenate([_split_heads_t(qiT_ref[j]) for j in range(N_PAIRS)], axis=1)

    def score_block(kb, carry):
        k0 = pl.multiple_of(kb * tk, tk)
        s = jnp.dot(ki_ref[pl.ds(k0, tk), :], qT_idx, preferred_element_type=F32)
        score = jnp.zeros((tk, tq), F32)
        for h in range(n_heads):
            score = score + w[h:h + 1, :] * jnp.maximum(s[:, h * tq:(h + 1) * tq], 0.0)
        sc_ref[pl.ds(k0, tk), :] = jnp.where(key_iota + k0 <= q_pos, score, -jnp.inf)
        return carry

    lax.fori_loop(0, nkb, score_block, 0)

    n_acc = 64

    def fold_keys(step, init, reduce):
        def blk(kb, a):
            k0 = pl.multiple_of(kb * tk, tk)
            return step(a, sc_ref[pl.ds(k0, tk), :].reshape(tk // n_acc, n_acc, tq), k0)
        a = lax.fori_loop(0, nkb, blk, jnp.full((n_acc, tq), init, F32))
        return reduce(a, axis=0, keepdims=True)

    def count(pred):
        return fold_keys(lambda a, x, k0: a + jnp.sum(jnp.where(pred(x, k0), 1.0, 0.0), axis=0),
                         0.0, jnp.sum)

    keep_all = (q_pos[0:1, :] + 1).astype(F32) <= topk
    row_max = fold_keys(lambda a, x, k0: jnp.maximum(a, jnp.max(x, axis=0)), -jnp.inf, jnp.max)
    row_min = fold_keys(
        lambda a, x, k0: jnp.minimum(a, jnp.min(jnp.where(x == -jnp.inf, jnp.inf, x), axis=0)),
        jnp.inf, jnp.min)

    RUN, DONE = 0.0, 1.0

    def probe(_, carry):
        lo, hi, state, thr = carry
        cand = lo + (hi - lo) * 0.5
        collapsed = (cand <= lo) | (cand >= hi)
        tot = count(lambda x, c0: x >= cand)
        running = state == RUN
        hit = running & (tot == topk)
        state = jnp.where(hit, DONE, jnp.where(running & collapsed, 2.0, state))
        thr = jnp.where(hit, cand, thr)
        return jnp.where(tot > topk, cand, lo), jnp.where(tot < topk, cand, hi), state, thr

    def n_running(state):
        return jnp.sum(jnp.where(state == RUN, 1.0, 0.0))

    def more_probes(carry):
        _, it, rest = carry
        rest = probe(0, probe(0, rest))
        return jnp.where(it + 2 >= MAX_PROBES, 0.0, n_running(rest[2])), it + 2, rest

    state0 = jnp.where(keep_all, DONE, RUN)
    hi0 = row_max + (jnp.abs(row_max) * 2.0 ** -20 + 1e-30)
    rest = lax.fori_loop(0, FIXED_PROBES, probe,
                         (row_min, hi0, state0, jnp.full((1, tq), -F32_MAX, F32)))
    _, _, (_, hi, state, thr) = lax.while_loop(
        lambda c: c[0] > 0.0, more_probes, (n_running(rest[2]), jnp.int32(FIXED_PROBES), rest))
    n_unresolved = jnp.sum(jnp.where(state == DONE, 0.0, 1.0))

    @pl.when(n_unresolved == 0.0)
    def _():
        def bias_block(kb, carry):
            k0 = pl.multiple_of(kb * tk, tk)
            sc_ref[pl.ds(k0, tk), :] = jnp.where(sc_ref[pl.ds(k0, tk), :] >= thr, 0.0, NEG)
            return carry
        lax.fori_loop(0, nkb, bias_block, 0)

    @pl.when(n_unresolved > 0.0)
    def _():
        def step_down(carry):
            _, hi, kth, found = carry
            below = fold_keys(
                lambda a, x, k0: jnp.maximum(a, jnp.max(jnp.where(x < hi, x, -jnp.inf), axis=0)),
                -jnp.inf, jnp.max)
            tot = count(lambda x, c0: x >= below)
            newly = (found == 0.0) & (tot >= topk)
            kth = jnp.where(newly, below, kth)
            found = jnp.where(newly, 1.0, found)
            return jnp.sum(1.0 - found), jnp.where(found == 0.0, below, hi), kth, found

        found0 = jnp.where(state == DONE, 1.0, 0.0)
        _, _, kth, _ = lax.while_loop(lambda c: c[0] > 0.0, step_down,
                                      (jnp.sum(1.0 - found0), hi, thr, found0))
        need = topk - count(lambda x, c0: x > kth)
        n_tie = count(lambda x, c0: x == kth)
        tie_ref[...] = jnp.full((1, tq), s_len, I32)

        @pl.when(jnp.sum(jnp.where((state != DONE) & (n_tie > need), 1.0, 0.0)) > 0.0)
        def _():
            n_bits = max(1, int(math.ceil(math.log2(s_len))))
            key_groups = key_iota.reshape(tk // n_acc, n_acc, tq)

            def tie_step(j, p):
                cand = p + lax.shift_left(jnp.int32(1), n_bits - 1 - j)
                tot = fold_keys(
                    lambda a, x, k0: a + jnp.sum(
                        jnp.where(x == kth, jnp.where(key_groups + k0 < cand, 1.0, 0.0), 0.0), axis=0),
                    0.0, jnp.sum)
                return jnp.where(tot < need, cand, p)

            tie_col = lax.fori_loop(0, n_bits, tie_step, jnp.zeros((1, tq), I32))
            tie_ref[...] = jnp.where(state == DONE, s_len, tie_col)

        tie_col = tie_ref[...]

        def bias_block(kb, carry):
            k0 = pl.multiple_of(kb * tk, tk)
            x = sc_ref[pl.ds(k0, tk), :]
            tie_bias = jnp.where(key_iota + k0 <= tie_col, 0.0, NEG)
            sc_ref[pl.ds(k0, tk), :] = jnp.where(x > kth, 0.0, jnp.where(x == kth, tie_bias, NEG))
            return carry
        lax.fori_loop(0, nkb, bias_block, 0)

    qT_s = jnp.concatenate([_split_heads_t(qaT_ref[j]) for j in range(N_PAIRS)], axis=1)

    def bias_of(kb, last):
        return sc_ref[pl.ds(pl.multiple_of(kb * tk, tk), tk), :]

    _skewed_flash_t(0, nkb, tk, qT_s, ka_ref, vaT_ref, bias_of, *flash_scratch,
                    bias_every_block=True)
    acc = flash_scratch[3][...]
    o = acc[0:HEAD_DIM] * (1.0 / acc[HEAD_DIM:HEAD_DIM + 1])
    for j in range(N_PAIRS):
        pair = jnp.concatenate([o[:, 2 * j * tq:(2 * j + 1) * tq],
                                o[:, (2 * j + 1) * tq:(2 * j + 2) * tq]], axis=0)
        o_ref[:, j * LANES:(j + 1) * LANES] = pair.T


def _dsa_attention(qiT, iwT, ki, qaT, ka, vaT, bsz, s_len, tq, tk):
    t_tokens = ki.shape[0]
    nq = s_len // tq
    topk = min(TOPK_MAX, s_len // 4)
    q_spec = pl.BlockSpec((N_PAIRS, LANES, tq), lambda b, i: (0, 0, b * nq + i))
    k_spec = pl.BlockSpec((s_len, LANES), lambda b, i: (b, 0))
    return pl.pallas_call(
        functools.partial(_dsa_kernel, tq, tk, float(topk), s_len),
        grid=(bsz, nq),
        in_specs=[
            q_spec,
            pl.BlockSpec((2 * N_PAIRS, tq), lambda b, i: (0, b * nq + i)),
            k_spec,
            q_spec,
            k_spec,
            pl.BlockSpec((1, vaT.shape[1], s_len), lambda b, i: (0, 0, b)),
        ],
        out_specs=pl.BlockSpec((tq, GROUP), lambda b, i: (b * nq + i, 0)),
        out_shape=jax.ShapeDtypeStruct((t_tokens, GROUP), F32),
        scratch_shapes=[pltpu.VMEM((s_len, tq), F32), pltpu.VMEM((1, tq), I32)]
        + _flash_t_scratch(tk, 2 * N_PAIRS * tq, vaT.shape[1]),
        compiler_params=_params("parallel", "parallel"),
        name="dsa_attention",
    )(qiT, iwT, ki, qaT, ka, vaT)


def _bias_tables(n, step, tq, tk, fn):
    d = np.arange(n)[:, None, None] * step + np.arange(tq)[None, :, None] - np.arange(tk)[None, None, :]
    return jnp.asarray(fn(d), F32)


def _window_kernel(tq, q_ref, kT_ref, v_ref, bias_ref, sink_ref, o_ref):
    j = pl.program_id(1)
    i = pl.program_id(2)
    q0 = i * tq
    wk = tq + B_WINDOW
    k0 = pl.multiple_of(jnp.maximum(q0 - B_WINDOW, 0), LANES)
    kT = kT_ref[0, :, pl.ds(k0, wk)]
    bias = bias_ref[(q0 - k0) // B_WINDOW]
    lane = lax.broadcasted_iota(I32, (tq, LANES), 1)
    s_both = jnp.dot(_stack_heads(q_ref), kT, preferred_element_type=F32)
    accs = []
    for half in range(2):
        sink = sink_ref[2 * j + half] * LOG2E
        s = s_both[half * tq:(half + 1) * tq] + bias
        m = jnp.maximum(jnp.max(s, axis=1, keepdims=True), sink)
        p = jnp.exp2(s - m).astype(BF16)
        acc = jnp.dot(p, v_ref[half, pl.ds(k0, wk), :], preferred_element_type=F32)
        accs.append(acc + jnp.where(lane == (HEAD_DIM if half == 0 else 0), jnp.exp2(sink - m), 0.0))
    o_ref[...] = _merge_heads(*accs)


def _window_attention(q, kT, v, sinks, bsz, s_len, tq):
    t_tokens = q.shape[0]
    nq = s_len // tq
    bias = _bias_tables(2, B_WINDOW, tq, tq + B_WINDOW,
                        lambda d: np.where((d >= 0) & (d < B_WINDOW), 0.0, NEG))
    return pl.pallas_call(
        functools.partial(_window_kernel, tq),
        grid=(bsz, N_PAIRS, nq),
        in_specs=[
            pl.BlockSpec((tq, LANES), lambda b, j, i: (b * nq + i, j)),
            pl.BlockSpec((1, LANES, s_len), lambda b, j, i: (j // 2, 0, b)),
            pl.BlockSpec((2, s_len, LANES), lambda b, j, i: (j // 2, b, 0)),
            pl.BlockSpec(bias.shape, lambda b, j, i: (0, 0, 0)),
            pl.BlockSpec(memory_space=pltpu.SMEM),
        ],
        out_specs=pl.BlockSpec((tq, LANES), lambda b, j, i: (b * nq + i, j)),
        out_shape=jax.ShapeDtypeStruct((t_tokens, GROUP), F32),
        compiler_params=_params("parallel", "parallel", "parallel"),
        name="window_attention",
    )(q, kT, v, bias, sinks)


def _dilated_log2_multiplicity(d):
    count = np.zeros(d.shape)
    for window, dilation in C_PATTERNS:
        count += (d >= 0) & (d <= window) & (d % dilation == 0)
    return np.where(count > 0, np.log2(np.maximum(count, 1)), NEG)


def _split_heads_t(qT):
    row = lax.broadcasted_iota(I32, qT.shape, 0)
    zero = jnp.zeros_like(qT)
    return jnp.concatenate([jnp.where(row < HEAD_DIM, qT, zero), jnp.where(row >= HEAD_DIM, qT, zero)],
                           axis=1)


def _skewed_flash_t(lo, hi, tk, qT_s, k_ref, vT_ref, bias_of, sa_ref, sb_ref, m_ref, acc_ref,
                    bias_every_block=False):
    m_ref[...] = jnp.full(m_ref.shape, NEG, F32)
    acc_ref[...] = jnp.zeros(acc_ref.shape, F32)

    def add_bias(s, bias):
        if bias is None:
            return s
        w = bias.shape[1]
        return jnp.concatenate([s[:, r * w:(r + 1) * w] + bias for r in range(s.shape[1] // w)],
                               axis=1)

    def logits_into(dst, kb):
        s = jnp.dot(k_ref[pl.ds(pl.multiple_of(kb * tk, tk), tk), :], qT_s,
                    preferred_element_type=F32)
        dst[...] = add_bias(s, bias_of(kb, False)) if bias_every_block else s

    def update_from(src, kb, last=False):
        s = src[...]
        if not bias_every_block:
            s = add_bias(s, bias_of(kb, last))
        m = m_ref[...]
        m_new = jnp.maximum(m, jnp.max(s, axis=0, keepdims=True))
        p = jnp.exp2(s - m_new).astype(BF16)
        vT = vT_ref[0, :, pl.ds(pl.multiple_of(kb * tk, tk), tk)]
        acc_ref[...] = jnp.exp2(m - m_new) * acc_ref[...] + jnp.dot(vT, p, preferred_element_type=F32)
        m_ref[...] = m_new

    n = hi - lo
    logits_into(sa_ref, lo)

    def pair(pp, carry):
        kb = lo + 2 * pp
        logits_into(sb_ref, kb + 1)
        update_from(sa_ref, kb)
        logits_into(sa_ref, kb + 2)
        update_from(sb_ref, kb + 1)
        return carry

    lax.fori_loop(0, (n - 1) // 2, pair, 0)

    @pl.when(n % 2 == 1)
    def _():
        update_from(sa_ref, hi - 1, last=True)

    @pl.when(n % 2 == 0)
    def _():
        logits_into(sb_ref, hi - 1)
        update_from(sa_ref, hi - 2)
        update_from(sb_ref, hi - 1, last=True)


def _flash_t_scratch(tk, n_queries, n_value_rows):
    return [pltpu.VMEM((tk, n_queries), F32), pltpu.VMEM((tk, n_queries), F32),
            pltpu.VMEM((1, n_queries), F32), pltpu.VMEM((n_value_rows, n_queries), F32)]


def _dilated_kernel(t, reach, qT_ref, k_ref, vT_ref, bias_ref, o_ref, *scratch):
    i = pl.program_id(2)
    q0 = i * t
    qT_s = _split_heads_t(qT_ref[0])

    def bias_of(kb, last):
        return bias_ref[(q0 - kb * t) // t]

    _skewed_flash_t(jnp.maximum(q0 - reach, 0) // t, i + 1, t, qT_s, k_ref, vT_ref, bias_of, *scratch,
                    bias_every_block=True)
    acc = scratch[3][...]
    w = HEAD_DIM + VT_PAD
    o = jnp.concatenate(
        [acc[0:HEAD_DIM, 0:t] * (1.0 / acc[HEAD_DIM:HEAD_DIM + 1, 0:t]),
         acc[w:w + HEAD_DIM, t:2 * t] * (1.0 / acc[w + HEAD_DIM:w + HEAD_DIM + 1, t:2 * t])], axis=0)
    o_ref[...] = o.T


def _dilated_attention(qT, k, vT, bsz, s_len, t):
    t_tokens = k.shape[0]
    nq = s_len // t
    reach = max(wd for wd, _ in C_PATTERNS)
    bias = jnp.swapaxes(_bias_tables((reach + t - 1) // t + 1, t, t, t, _dilated_log2_multiplicity),
                        1, 2)
    return pl.pallas_call(
        functools.partial(_dilated_kernel, t, reach),
        grid=(bsz, N_PAIRS, nq),
        in_specs=[
            pl.BlockSpec((1, LANES, t), lambda b, j, i: (j, 0, b * nq + i)),
            pl.BlockSpec((s_len, LANES), lambda b, j, i: (b, j)),
            pl.BlockSpec((1, vT.shape[1], s_len), lambda b, j, i: (j, 0, b)),
            pl.BlockSpec(bias.shape, lambda b, j, i: (0, 0, 0)),
        ],
        out_specs=pl.BlockSpec((t, LANES), lambda b, j, i: (b * nq + i, j)),
        out_shape=jax.ShapeDtypeStruct((t_tokens, GROUP), F32),
        scratch_shapes=_flash_t_scratch(t, 2 * t, vT.shape[1]),
        compiler_params=_params("parallel", "parallel", "parallel"),
        name="dilated_attention",
    )(qT, k, vT, bias)


def _diff_kernel(t, lambda_init, qT_ref, k_ref, vT_ref, bias_ref, lq1_ref, lk1_ref, lq2_ref,
                 lk2_ref, sg_ref, o_ref, *scratch):
    i = pl.program_id(2)
    qT_s = _split_heads_t(qT_ref[0])

    def bias_of(kb, last):
        return bias_ref[...] if last else None

    _skewed_flash_t(0, i + 1, t, qT_s, k_ref, vT_ref, bias_of, *scratch)
    acc = scratch[3][...]
    o = acc[0:LANES] * (1.0 / acc[LANES:LANES + 1])
    lam = (jnp.exp(jnp.sum(lq1_ref[...] * lk1_ref[...], axis=1, keepdims=True))
           - jnp.exp(jnp.sum(lq2_ref[...] * lk2_ref[...], axis=1, keepdims=True)) + lambda_init)
    o = o[:, 0:t] - lam * o[:, t:2 * t]
    ms = jnp.mean(o * o, axis=0, keepdims=True)
    o_ref[...] = ((o * lax.rsqrt(ms + NORM_EPS) * sg_ref[...]) * (1.0 - lambda_init)).T


def _diff_attention(qT, k, vT, lq1, lk1, lq2, lk2, sub_gain, lambda_init, bsz, s_len, t):
    t_tokens = k.shape[0]
    nq = s_len // t
    bias = _bias_tables(1, t, t, t, lambda d: np.where(d >= 0, 0.0, NEG))[0].T
    vec = lambda a: a.reshape(1, -1).astype(F32)
    small = lambda n: pl.BlockSpec((1, n), lambda b, j, i: (0, 0))
    return pl.pallas_call(
        functools.partial(_diff_kernel, t, lambda_init),
        grid=(bsz, N_PAIRS, nq),
        in_specs=[
            pl.BlockSpec((1, LANES, t), lambda b, j, i: (j, 0, b * nq + i)),
            pl.BlockSpec((s_len, LANES), lambda b, j, i: (b, j)),
            pl.BlockSpec((1, vT.shape[1], s_len), lambda b, j, i: (j, 0, b)),
            pl.BlockSpec((t, t), lambda b, j, i: (0, 0)),
            small(HEAD_DIM), small(HEAD_DIM), small(HEAD_DIM), small(HEAD_DIM),
            pl.BlockSpec((LANES, 1), lambda b, j, i: (0, 0)),
        ],
        out_specs=pl.BlockSpec((t, LANES), lambda b, j, i: (b * nq + i, j)),
        out_shape=jax.ShapeDtypeStruct((t_tokens, GROUP), F32),
        scratch_shapes=_flash_t_scratch(t, 2 * t, vT.shape[1]),
        compiler_params=_params("parallel", "parallel", "parallel"),
        name="diff_attention",
    )(qT, k, vT, bias, vec(lq1), vec(lk1), vec(lq2), vec(lk2), sub_gain.reshape(-1, 1).astype(F32))


def _outproj_kernel(o1_ref, o2_ref, g_ref, x_ref, p_ref, wo_ref, pg_ref, wg_ref, wp_ref, out_ref):
    g = g_ref[...]
    y1 = (o1_ref[...] * g[:, 0:GROUP]).astype(BF16)
    y2 = (o2_ref[...] * g[:, GROUP:2 * GROUP]).astype(BF16)
    x1 = (x_ref[...]
          + jnp.dot(y1, wo_ref[0:GROUP, :], preferred_element_type=F32)
          + jnp.dot(y2, wo_ref[GROUP:2 * GROUP, :], preferred_element_type=F32))
    ms = jnp.mean(x1 * x1, axis=-1, keepdims=True)
    hn = (x1 * lax.rsqrt(ms + NORM_EPS) * pg_ref[...]).astype(BF16)
    z = jnp.dot(hn, wg_ref[...], preferred_element_type=F32)
    gate = 1.0 / (1.0 + jnp.exp(-z))
    pp = jnp.dot(p_ref[...].astype(BF16), wp_ref[...], preferred_element_type=F32)
    out_ref[...] = x1 + pp * gate


def _outproj(o1, o2, gate, x2, p2, w_out, ple_gain, w_gate, w_proj, tm):
    t_tokens = x2.shape[0]
    tok = lambda n: pl.BlockSpec((tm, n), lambda i: (i, 0))
    full = lambda a: pl.BlockSpec(a.shape, lambda i: (0, 0))
    pg = ple_gain.reshape(1, D_MODEL)
    return pl.pallas_call(
        _outproj_kernel,
        grid=(t_tokens // tm,),
        in_specs=[tok(GROUP), tok(GROUP), tok(2 * GROUP), tok(D_MODEL), tok(PLE_DIM),
                  full(w_out), full(pg), full(w_gate), full(w_proj)],
        out_specs=tok(D_MODEL),
        out_shape=jax.ShapeDtypeStruct((t_tokens, D_MODEL), F32),
        compiler_params=_params("parallel"),
        name="outproj_ple",
    )(o1, o2, gate, x2, p2, w_out, pg, w_gate, w_proj)


TM = 256
TQ_DSA = 128
TK_DSA = 512
TQ_WINDOW = 256
T_DILATED = 512
T_DIFF = 512


def _rope_tables(s_len):
    half = HEAD_DIM // 2
    inv = ROPE_THETA ** (-jnp.arange(half, dtype=F32) / half)
    ang = jnp.arange(s_len).astype(F32)[:, None] * inv[None, :]
    cos = jnp.tile(jnp.cos(ang), (1, 4))
    sin = jnp.sin(ang)
    sin_signed = jnp.tile(jnp.concatenate([-sin, sin], axis=1), (1, 2))
    return cos, sin_signed


def _pair_gain(g):
    return jnp.concatenate([g, g]).astype(F32)


def _even_layer(x2, p2, bsz, s_len, tables, norm_gain, w_in, w_out, a_q_gain, a_k_gain, idx_k_gain,
                b_q_gain, b_k_gain, b_sinks, ple_gain, w_gate, w_proj):
    hd = HEAD_DIM
    offs = np.cumsum([0, 512, hd, hd, 512, hd, 8, 512, 512, 2 * hd, 2 * hd, 512])
    aq, ak, av, iq, ik, iw, ag, bq, bk, bv, bg = [w_in[:, offs[n]:offs[n + 1]] for n in range(11)]
    dup = lambda c: jnp.concatenate([c, c], axis=1)
    w = jnp.concatenate(
        [aq, iq, bq, ag, bg,
         dup(ak), dup(ik), dup(bk[:, :hd]), dup(bk[:, hd:]),
         dup(av), dup(bv[:, :hd]), dup(bv[:, hd:]),
         jnp.pad(iw, ((0, 0), (0, LANES - 8)))], axis=1).astype(BF16)
    ones = jnp.ones((LANES,), F32)
    gains = jnp.stack(
        [_pair_gain(a_q_gain)] * 4 + [ones] * 4 + [_pair_gain(b_q_gain)] * 4 + [ones] * 8
        + [_pair_gain(a_k_gain), _pair_gain(idx_k_gain), _pair_gain(b_k_gain), _pair_gain(b_k_gain)]
        + [ones] * 4)
    idx_scale = (8 * hd) ** -0.5
    plan = ([("k", 0, r, True, QSCALE) for r in range(4)]
            + [("k", 1, r, False, 1.0) for r in range(4)]
            + [("q", 2, c, True, QSCALE) for c in range(4)]
            + [("silu", 3, c) for c in range(8)]
            + [("q", 4, 0, True, 1.0)]
            + [("q", 5, 0, True, 1.0)]
            + [("k", 6, r, True, 1.0) for r in range(2)]
            + [("vt", 7, 0, HEAD_DIM, 1)]
            + [("v2", 8, 2 * r) for r in range(2)]
            + [("iw", 9, idx_scale)])
    out_defs = [("kt", 4, BF16), ("kt", 4, BF16), ("tok", 512, BF16), ("tok", 1024, F32),
                ("tok", LANES, BF16), ("tok", LANES, BF16), ("kt", 2, BF16),
                ("vt", (1, HEAD_DIM + VT_PAD), BF16), ("v", 4, BF16), ("rows", 8, F32)]
    qaT, qiT, qb, gate, ka, ki, kbT, vaT, vb, iwT = _inproj(x2, norm_gain, w, gains, *tables, plan,
                                                            out_defs, TM)
    oa = _dsa_attention(qiT, iwT, ki, qaT, ka, vaT, bsz, s_len, min(TQ_DSA, s_len), TK_DSA)
    ob = _window_attention(qb, kbT, vb, b_sinks.astype(F32), bsz, s_len, TQ_WINDOW)
    return _outproj(oa, ob, gate, x2, p2, w_out.astype(BF16), ple_gain, w_gate.astype(BF16),
                    w_proj.astype(BF16), TM)


def _odd_layer(x2, p2, bsz, s_len, tables, norm_gain, w_in, w_out, c_q_gain, c_k_gain, d_q_gain,
               d_k_gain, lq1, lk1, lq2, lk2, sub_gain, lambda_init, ple_gain, w_gate, w_proj):
    ones = jnp.ones((LANES,), F32)
    gains = jnp.stack([_pair_gain(c_q_gain)] * 4 + [_pair_gain(c_k_gain)] * 4 + [ones] * 8
                      + [_pair_gain(d_q_gain)] * 4 + [_pair_gain(d_k_gain)] * 4 + [ones] * 8)
    plan = ([("k", 0, r, True, QSCALE) for r in range(4)]
            + [("q", 1, c, True, 1.0) for c in range(4)]
            + [("vt", 2, r, HEAD_DIM, 2) for r in range(4)]
            + [("silu", 3, c) for c in range(4)]
            + [("k", 4, r, True, QSCALE) for r in range(4)]
            + [("q", 5, c, True, 1.0) for c in range(4)]
            + [("vt", 6, r, 2 * HEAD_DIM, 1) for r in range(4)]
            + [("silu", 3, 4 + c) for c in range(4)])
    out_defs = [("kt", 4, BF16), ("tok", 512, BF16), ("vt", (4, 2 * (HEAD_DIM + VT_PAD)), BF16),
                ("tok", 1024, F32),
                ("kt", 4, BF16), ("tok", 512, BF16), ("vt", (4, 2 * HEAD_DIM + VT_PAD), BF16)]
    qcT, kc, vcT, gate, qdT, kd, vdT = _inproj(x2, norm_gain, w_in.astype(BF16), gains, *tables,
                                               plan, out_defs, TM)
    oc = _dilated_attention(qcT, kc, vcT, bsz, s_len, T_DILATED)
    od = _diff_attention(qdT, kd, vdT, lq1, lk1, lq2, lk2, sub_gain, lambda_init, bsz, s_len, T_DIFF)
    return _outproj(oc, od, gate, x2, p2, w_out.astype(BF16), ple_gain, w_gate.astype(BF16),
                    w_proj.astype(BF16), TM)


def kernel(x, p, norm_gain, w_in_even, w_out_even, a_q_gain, a_k_gain, idx_k_gain, b_q_gain, b_k_gain, b_sinks, w_in_odd, w_out_odd, c_q_gain, c_k_gain, d_q_gain, d_k_gain, d_lambda_q1, d_lambda_k1, d_lambda_q2, d_lambda_k2, d_subln_gain, ple_norm_gain, w_ple_gate, w_ple_proj):
    bsz, s_len, d_model = x.shape
    depth = p.shape[0]
    assert d_model == D_MODEL and s_len % 512 == 0
    tables = _rope_tables(s_len)
    x2 = x.reshape(bsz * s_len, d_model)
    for i in range(depth):
        j = i // 2
        p2 = p[i].reshape(bsz * s_len, PLE_DIM)
        if i % 2 == 0:
            x2 = _even_layer(x2, p2, bsz, s_len, tables, norm_gain[i], w_in_even[j], w_out_even[j],
                             a_q_gain[j], a_k_gain[j], idx_k_gain[j], b_q_gain[j], b_k_gain[j],
                             b_sinks[j], ple_norm_gain[i], w_ple_gate[i], w_ple_proj[i])
        else:
            lambda_init = 0.8 - 0.6 * math.exp(-0.3 * i)
            x2 = _odd_layer(x2, p2, bsz, s_len, tables, norm_gain[i], w_in_odd[j], w_out_odd[j],
                            c_q_gain[j], c_k_gain[j], d_q_gain[j], d_k_gain[j], d_lambda_q1[j],
                            d_lambda_k1[j], d_lambda_q2[j], d_lambda_k2[j], d_subln_gain[j],
                            lambda_init, ple_norm_gain[i], w_ple_gate[i], w_ple_proj[i])
    return x2.reshape(bsz, s_len, d_model)
```

```python
import functools
import math

import numpy as np
import jax
import jax.numpy as jnp
from jax import lax
from jax.experimental import pallas as pl
from jax.experimental.pallas import tpu as pltpu

F32 = jnp.float32
BF16 = jnp.bfloat16
I32 = jnp.int32

D_MODEL = 1024
HEAD_DIM = 64
ROPE_THETA = 10000.0
NORM_EPS = 1e-6
PLE_DIM = 256
TOPK_MAX = 256
B_WINDOW = 128
C_PATTERNS = ((128, 1), (512, 4), (2048, 16))
N_PAIRS = 4
LANES = 128
GROUP = 512
VT_PAD = 16
LOG2E = 1.4426950408889634
QSCALE = HEAD_DIM ** -0.5 * LOG2E
NEG = -1e30
F32_MAX = float(np.finfo(np.float32).max)
FIXED_PROBES = 16
MAX_PROBES = 20
VMEM_LIMIT = 48 * 1024 * 1024


def _params(*sem):
    return pltpu.CompilerParams(dimension_semantics=sem, vmem_limit_bytes=VMEM_LIMIT)


def _norm_rope(y, gain, cos, sin_signed, bd, first_half, norm):
    if norm:
        ss = y * y
        hi = ss.astype(BF16)
        lo = (ss - hi.astype(F32)).astype(BF16)
        seg = (jnp.dot(hi, bd, preferred_element_type=F32)
               + jnp.dot(lo, bd, preferred_element_type=F32))
        y = y * lax.rsqrt(seg * (1.0 / HEAD_DIM) + NORM_EPS) * gain
    rot = jnp.where(first_half, pltpu.roll(y, 96, 1), pltpu.roll(y, 32, 1))
    return y * cos + rot * sin_signed


def _inproj_kernel(plan, x_ref, ng_ref, w_ref, cos_ref, sin_ref, gains_ref, bd_ref, *outs):
    x = x_ref[...]
    ms = jnp.mean(x * x, axis=-1, keepdims=True)
    h = (x * lax.rsqrt(ms + NORM_EPS) * ng_ref[...]).astype(BF16)
    cos = cos_ref[...]
    sin = sin_ref[...]
    bd = bd_ref[...]
    tm = x.shape[0]
    lane = lax.broadcasted_iota(I32, (tm, LANES), 1)
    lo_half = lane < HEAD_DIM
    first_half = (lane & (HEAD_DIM - 1)) < HEAD_DIM // 2
    one_at_64 = jnp.where(lane == HEAD_DIM, 1.0, 0.0)
    one_at_0 = jnp.where(lane == 0, 1.0, 0.0)
    for g in range(len(plan) // 4):
        yg = jnp.dot(h, w_ref[:, g * GROUP:(g + 1) * GROUP], preferred_element_type=F32)
        for c4 in range(4):
            c = g * 4 + c4
            y = yg[:, c4 * LANES:(c4 + 1) * LANES]
            step = plan[c]
            kind, out = step[0], outs[step[1]]
            if kind == "q":
                _, _, col, norm, mult = step
                y = _norm_rope(y, gains_ref[c:c + 1, :], cos, sin, bd, first_half, norm)
                out[:, col * LANES:(col + 1) * LANES] = (y * mult).astype(BF16)
            elif kind == "k":
                _, _, row, norm, mult = step
                y = _norm_rope(y, gains_ref[c:c + 1, :], cos, sin, bd, first_half, norm)
                out[row] = (y * mult).T.astype(BF16)
            elif kind == "vt":
                _, _, row, head_dim, n_heads = step
                yT = y.T.astype(BF16)
                sub = lax.broadcasted_iota(I32, (VT_PAD, tm), 0)
                ones_row = jnp.where(sub == 0, 1.0, 0.0).astype(BF16)
                for n in range(n_heads):
                    base = n * (head_dim + VT_PAD)
                    out[row, base:base + head_dim, :] = yT[n * head_dim:(n + 1) * head_dim]
                    out[row, base + head_dim:base + head_dim + VT_PAD, :] = ones_row
            elif kind == "silu":
                _, _, col = step
                out[:, col * LANES:(col + 1) * LANES] = y * (1.0 / (1.0 + jnp.exp(-y)))
            elif kind == "v2":
                _, _, row = step
                out[row] = jnp.where(lo_half, y, one_at_64).astype(BF16)
                out[row + 1] = jnp.where(lo_half, one_at_0, y).astype(BF16)
            elif kind == "v1":
                _, _, row = step
                out[row, :, 0:LANES] = y.astype(BF16)
                out[row, :, LANES:2 * LANES] = one_at_0.astype(BF16)
            elif kind == "iw":
                _, _, mult = step
                out[...] = (y * mult).T[0:8]
            else:
                raise ValueError(kind)


def _inproj(x2, norm_gain, w, gains, cos_t, sin_t, plan, out_defs, tm):
    t_tokens = x2.shape[0]
    s_len = cos_t.shape[0]
    n_cols = w.shape[1]
    n_tiles = t_tokens // tm
    s_tiles = s_len // tm
    bd = jnp.asarray(np.kron(np.eye(2), np.ones((HEAD_DIM, HEAD_DIM))), BF16)
    out_shapes, out_specs = [], []
    for kind, n, dtype in out_defs:
        if kind == "tok":
            out_shapes.append(jax.ShapeDtypeStruct((t_tokens, n), dtype))
            out_specs.append(pl.BlockSpec((tm, n), lambda i: (i, 0)))
        elif kind == "kt":
            out_shapes.append(jax.ShapeDtypeStruct((n, LANES, t_tokens), dtype))
            out_specs.append(pl.BlockSpec((n, LANES, tm), lambda i: (0, 0, i)))
        elif kind == "v":
            out_shapes.append(jax.ShapeDtypeStruct((n, t_tokens, LANES), dtype))
            out_specs.append(pl.BlockSpec((n, tm, LANES), lambda i: (0, i, 0)))
        elif kind == "vt":
            out_shapes.append(jax.ShapeDtypeStruct((n[0], n[1], t_tokens), dtype))
            out_specs.append(pl.BlockSpec((n[0], n[1], tm), lambda i: (0, 0, i)))
        elif kind == "rows":
            out_shapes.append(jax.ShapeDtypeStruct((n, t_tokens), dtype))
            out_specs.append(pl.BlockSpec((n, tm), lambda i: (0, i)))
        elif kind == "vw":
            out_shapes.append(jax.ShapeDtypeStruct((n, t_tokens, 2 * LANES), dtype))
            out_specs.append(pl.BlockSpec((n, tm, 2 * LANES), lambda i: (0, i, 0)))
    return pl.pallas_call(
        functools.partial(_inproj_kernel, plan),
        grid=(n_tiles,),
        in_specs=[
            pl.BlockSpec((tm, D_MODEL), lambda i: (i, 0)),
            pl.BlockSpec((1, D_MODEL), lambda i: (0, 0)),
            pl.BlockSpec((D_MODEL, n_cols), lambda i: (0, 0)),
            pl.BlockSpec((tm, LANES), lambda i: (i % s_tiles, 0)),
            pl.BlockSpec((tm, LANES), lambda i: (i % s_tiles, 0)),
            pl.BlockSpec(gains.shape, lambda i: (0, 0)),
            pl.BlockSpec((LANES, LANES), lambda i: (0, 0)),
        ],
        out_specs=out_specs,
        out_shape=out_shapes,
        compiler_params=_params("parallel"),
        name="inproj",
    )(x2, norm_gain.reshape(1, D_MODEL), w, cos_t, sin_t, gains, bd)


def _split_heads(qc):
    lane = lax.broadcasted_iota(I32, qc.shape, 1)
    zero = jnp.zeros_like(qc)
    return jnp.where(lane < HEAD_DIM, qc, zero), jnp.where(lane >= HEAD_DIM, qc, zero)


def _stack_heads(q_ref):
    tq, width = q_ref.shape
    parts = []
    for j in range(width // LANES):
        parts.extend(_split_heads(q_ref[:, j * LANES:(j + 1) * LANES]))
    return jnp.concatenate(parts, axis=0)


def _merge_heads(acc_lo, acc_hi):
    lane = lax.broadcasted_iota(I32, acc_lo.shape, 1)
    o_lo = acc_lo * (1.0 / acc_lo[:, HEAD_DIM:HEAD_DIM + 1])
    o_hi = acc_hi * (1.0 / acc_hi[:, 0:1])
    return jnp.where(lane < HEAD_DIM, o_lo, o_hi)


def _dsa_kernel(tq, tk, topk, s_len, qiT_ref, iwT_ref, ki_ref, qaT_ref, ka_ref, vaT_ref, o_ref,
                sc_ref, *flash_scratch):
    i = pl.program_id(1)
    q0 = i * tq
    nkb = (q0 + tq + tk - 1) // tk
    n_heads = 2 * N_PAIRS
    key_iota = lax.broadcasted_iota(I32, (tk, tq), 0)
    q_pos = q0 + lax.broadcasted_iota(I32, (tk, tq), 1)

    w = iwT_ref[...]
    qT_idx = jnp.concatenate([_split_heads_t(qiT_ref[j]) for j in range(N_PAIRS)], axis=1)

    def score_block(kb, carry):
        k0 = pl.multiple_of(kb * tk, tk)
        s = jnp.dot(ki_ref[pl.ds(k0, tk), :], qT_idx, preferred_element_type=F32)
        score = jnp.zeros((tk, tq), F32)
        for h in range(n_heads):
            score = score + w[h:h + 1, :] * jnp.maximum(s[:, h * tq:(h + 1) * tq], 0.0)
        sc_ref[pl.ds(k0, tk), :] = jnp.where(key_iota + k0 <= q_pos, score, -jnp.inf)
        return carry

    lax.fori_loop(0, nkb, score_block, 0)

    n_acc = 64

    def fold_keys(step, init, reduce):
        def blk(kb, a):
            k0 = pl.multiple_of(kb * tk, tk)
            return step(a, sc_ref[pl.ds(k0, tk), :].reshape(tk // n_acc, n_acc, tq), k0)
        a = lax.fori_loop(0, nkb, blk, jnp.full((n_acc, tq), init, F32))
        return reduce(a, axis=0, keepdims=True)

    def count(pred):
        return fold_keys(lambda a, x, k0: a + jnp.sum(jnp.where(pred(x, k0), 1.0, 0.0), axis=0),
                         0.0, jnp.sum)

    keep_all = (q_pos[0:1, :] + 1).astype(F32) <= topk
    row_max = fold_keys(lambda a, x, k0: jnp.maximum(a, jnp.max(x, axis=0)), -jnp.inf, jnp.max)
    row_min = fold_keys(
        lambda a, x, k0: jnp.minimum(a, jnp.min(jnp.where(x == -jnp.inf, jnp.inf, x), axis=0)),
        jnp.inf, jnp.min)

    RUN, DONE = 0.0, 1.0

    def probe(_, carry):
        lo, hi, state, thr = carry
        cand = lo + (hi - lo) * 0.5
        collapsed = (cand <= lo) | (cand >= hi)
        tot = count(lambda x, c0: x >= cand)
        running = state == RUN
        hit = running & (tot == topk)
        state = jnp.where(hit, DONE, jnp.where(running & collapsed, 2.0, state))
        thr = jnp.where(hit, cand, thr)
        return jnp.where(tot > topk, cand, lo), jnp.where(tot < topk, cand, hi), state, thr

    def n_running(state):
        return jnp.sum(jnp.where(state == RUN, 1.0, 0.0))

    def more_probes(carry):
        _, it, rest = carry
        rest = probe(0, probe(0, rest))
        return jnp.where(it + 2 >= MAX_PROBES, 0.0, n_running(rest[2])), it + 2, rest

    state0 = jnp.where(keep_all, DONE, RUN)
    hi0 = row_max + (jnp.abs(row_max) * 2.0 ** -20 + 1e-30)
    rest = lax.fori_loop(0, FIXED_PROBES, probe,
                         (row_min, hi0, state0, jnp.full((1, tq), -F32_MAX, F32)))
    _, _, (_, hi, state, thr) = lax.while_loop(
        lambda c: c[0] > 0.0, more_probes, (n_running(rest[2]), jnp.int32(FIXED_PROBES), rest))
    n_unresolved = jnp.sum(jnp.where(state == DONE, 0.0, 1.0))

    @pl.when(n_unresolved == 0.0)
    def _():
        def bias_block(kb, carry):
            k0 = pl.multiple_of(kb * tk, tk)
            sc_ref[pl.ds(k0, tk), :] = jnp.where(sc_ref[pl.ds(k0, tk), :] >= thr, 0.0, NEG)
            return carry
        lax.fori_loop(0, nkb, bias_block, 0)

    @pl.when(n_unresolved > 0.0)
    def _():
        def step_down(carry):
            _, hi, kth, found = carry
            below = fold_keys(
                lambda a, x, k0: jnp.maximum(a, jnp.max(jnp.where(x < hi, x, -jnp.inf), axis=0)),
                -jnp.inf, jnp.max)
            tot = count(lambda x, c0: x >= below)
            newly = (found == 0.0) & (tot >= topk)
            kth = jnp.where(newly, below, kth)
            found = jnp.where(newly, 1.0, found)
            return jnp.sum(1.0 - found), jnp.where(found == 0.0, below, hi), kth, found

        found0 = jnp.where(state == DONE, 1.0, 0.0)
        _, _, kth, _ = lax.while_loop(lambda c: c[0] > 0.0, step_down,
                                      (jnp.sum(1.0 - found0), hi, thr, found0))
        need = topk - count(lambda x, c0: x > kth)
        tri = jnp.where(lax.broadcasted_iota(I32, (tk, tk), 1) <= lax.broadcasted_iota(I32, (tk, tk), 0),
                        1.0, 0.0).astype(BF16)

        def bias_block(kb, ties_before):
            k0 = pl.multiple_of(kb * tk, tk)
            x = sc_ref[pl.ds(k0, tk), :]
            is_tie = x == kth
            rank = ties_before + jnp.dot(tri, jnp.where(is_tie, 1.0, 0.0).astype(BF16),
                                         preferred_element_type=F32)
            tie_bias = jnp.where(rank <= need, 0.0, NEG)
            sc_ref[pl.ds(k0, tk), :] = jnp.where(x > kth, 0.0, jnp.where(is_tie, tie_bias, NEG))
            return rank[tk - 1:tk, :]
        lax.fori_loop(0, nkb, bias_block, jnp.zeros((1, tq), F32))

    qT_s = jnp.concatenate([_split_heads_t(qaT_ref[j]) for j in range(N_PAIRS)], axis=1)

    def bias_of(kb, last):
        return sc_ref[pl.ds(pl.multiple_of(kb * tk, tk), tk), :]

    _skewed_flash_t(0, nkb, tk, qT_s, ka_ref, vaT_ref, bias_of, *flash_scratch,
                    bias_every_block=True)
    acc = flash_scratch[3][...]
    o = acc[0:HEAD_DIM] * (1.0 / acc[HEAD_DIM:HEAD_DIM + 1])
    for j in range(N_PAIRS):
        pair = jnp.concatenate([o[:, 2 * j * tq:(2 * j + 1) * tq],
                                o[:, (2 * j + 1) * tq:(2 * j + 2) * tq]], axis=0)
        o_ref[:, j * LANES:(j + 1) * LANES] = pair.T


def _dsa_attention(qiT, iwT, ki, qaT, ka, vaT, bsz, s_len, tq, tk):
    t_tokens = ki.shape[0]
    nq = s_len // tq
    topk = min(TOPK_MAX, s_len // 4)
    q_spec = pl.BlockSpec((N_PAIRS, LANES, tq), lambda b, i: (0, 0, b * nq + i))
    k_spec = pl.BlockSpec((s_len, LANES), lambda b, i: (b, 0))
    return pl.pallas_call(
        functools.partial(_dsa_kernel, tq, tk, float(topk), s_len),
        grid=(bsz, nq),
        in_specs=[
            q_spec,
            pl.BlockSpec((2 * N_PAIRS, tq), lambda b, i: (0, b * nq + i)),
            k_spec,
            q_spec,
            k_spec,
            pl.BlockSpec((1, vaT.shape[1], s_len), lambda b, i: (0, 0, b)),
        ],
        out_specs=pl.BlockSpec((tq, GROUP), lambda b, i: (b * nq + i, 0)),
        out_shape=jax.ShapeDtypeStruct((t_tokens, GROUP), F32),
        scratch_shapes=[pltpu.VMEM((s_len, tq), F32)]
        + _flash_t_scratch(tk, 2 * N_PAIRS * tq, vaT.shape[1]),
        compiler_params=_params("parallel", "parallel"),
        name="dsa_attention",
    )(qiT, iwT, ki, qaT, ka, vaT)


def _bias_tables(n, step, tq, tk, fn):
    d = np.arange(n)[:, None, None] * step + np.arange(tq)[None, :, None] - np.arange(tk)[None, None, :]
    return jnp.asarray(fn(d), F32)


def _window_kernel(tq, q_ref, kT_ref, v_ref, bias_ref, sink_ref, o_ref):
    j = pl.program_id(1)
    i = pl.program_id(2)
    q0 = i * tq
    wk = tq + B_WINDOW
    k0 = pl.multiple_of(jnp.maximum(q0 - B_WINDOW, 0), LANES)
    kT = kT_ref[0, :, pl.ds(k0, wk)]
    bias = bias_ref[(q0 - k0) // B_WINDOW]
    lane = lax.broadcasted_iota(I32, (tq, LANES), 1)
    s_both = jnp.dot(_stack_heads(q_ref), kT, preferred_element_type=F32)
    accs = []
    for half in range(2):
        sink = sink_ref[2 * j + half] * LOG2E
        s = s_both[half * tq:(half + 1) * tq] + bias
        m = jnp.maximum(jnp.max(s, axis=1, keepdims=True), sink)
        p = jnp.exp2(s - m).astype(BF16)
        acc = jnp.dot(p, v_ref[half, pl.ds(k0, wk), :], preferred_element_type=F32)
        accs.append(acc + jnp.where(lane == (HEAD_DIM if half == 0 else 0), jnp.exp2(sink - m), 0.0))
    o_ref[...] = _merge_heads(*accs)


def _window_attention(q, kT, v, sinks, bsz, s_len, tq):
    t_tokens = q.shape[0]
    nq = s_len // tq
    bias = _bias_tables(2, B_WINDOW, tq, tq + B_WINDOW,
                        lambda d: np.where((d >= 0) & (d < B_WINDOW), 0.0, NEG))
    return pl.pallas_call(
        functools.partial(_window_kernel, tq),
        grid=(bsz, N_PAIRS, nq),
        in_specs=[
            pl.BlockSpec((tq, LANES), lambda b, j, i: (b * nq + i, j)),
            pl.BlockSpec((1, LANES, s_len), lambda b, j, i: (j // 2, 0, b)),
            pl.BlockSpec((2, s_len, LANES), lambda b, j, i: (j // 2, b, 0)),
            pl.BlockSpec(bias.shape, lambda b, j, i: (0, 0, 0)),
            pl.BlockSpec(memory_space=pltpu.SMEM),
        ],
        out_specs=pl.BlockSpec((tq, LANES), lambda b, j, i: (b * nq + i, j)),
        out_shape=jax.ShapeDtypeStruct((t_tokens, GROUP), F32),
        compiler_params=_params("parallel", "parallel", "parallel"),
        name="window_attention",
    )(q, kT, v, bias, sinks)


def _dilated_log2_multiplicity(d):
    count = np.zeros(d.shape)
    for window, dilation in C_PATTERNS:
        count += (d >= 0) & (d <= window) & (d % dilation == 0)
    return np.where(count > 0, np.log2(np.maximum(count, 1)), NEG)


def _split_heads_t(qT):
    row = lax.broadcasted_iota(I32, qT.shape, 0)
    zero = jnp.zeros_like(qT)
    return jnp.concatenate([jnp.where(row < HEAD_DIM, qT, zero), jnp.where(row >= HEAD_DIM, qT, zero)],
                           axis=1)


def _skewed_flash_t(lo, hi, tk, qT_s, k_ref, vT_ref, bias_of, sa_ref, sb_ref, m_ref, acc_ref,
                    bias_every_block=False):
    m_ref[...] = jnp.full(m_ref.shape, NEG, F32)
    acc_ref[...] = jnp.zeros(acc_ref.shape, F32)

    def add_bias(s, bias):
        if bias is None:
            return s
        w = bias.shape[1]
        return jnp.concatenate([s[:, r * w:(r + 1) * w] + bias for r in range(s.shape[1] // w)],
                               axis=1)

    def logits_into(dst, kb):
        s = jnp.dot(k_ref[pl.ds(pl.multiple_of(kb * tk, tk), tk), :], qT_s,
                    preferred_element_type=F32)
        dst[...] = add_bias(s, bias_of(kb, False)) if bias_every_block else s

    def update_from(src, kb, last=False):
        s = src[...]
        if not bias_every_block:
            s = add_bias(s, bias_of(kb, last))
        m = m_ref[...]
        m_new = jnp.maximum(m, jnp.max(s, axis=0, keepdims=True))
        p = jnp.exp2(s - m_new).astype(BF16)
        vT = vT_ref[0, :, pl.ds(pl.multiple_of(kb * tk, tk), tk)]
        acc_ref[...] = jnp.exp2(m - m_new) * acc_ref[...] + jnp.dot(vT, p, preferred_element_type=F32)
        m_ref[...] = m_new

    n = hi - lo
    logits_into(sa_ref, lo)

    def pair(pp, carry):
        kb = lo + 2 * pp
        logits_into(sb_ref, kb + 1)
        update_from(sa_ref, kb)
        logits_into(sa_ref, kb + 2)
        update_from(sb_ref, kb + 1)
        return carry

    lax.fori_loop(0, (n - 1) // 2, pair, 0)

    @pl.when(n % 2 == 1)
    def _():
        update_from(sa_ref, hi - 1, last=True)

    @pl.when(n % 2 == 0)
    def _():
        logits_into(sb_ref, hi - 1)
        update_from(sa_ref, hi - 2)
        update_from(sb_ref, hi - 1, last=True)


def _flash_t_scratch(tk, n_queries, n_value_rows):
    return [pltpu.VMEM((tk, n_queries), F32), pltpu.VMEM((tk, n_queries), F32),
            pltpu.VMEM((1, n_queries), F32), pltpu.VMEM((n_value_rows, n_queries), F32)]


def _dilated_kernel(t, reach, qT_ref, k_ref, vT_ref, bias_ref, o_ref, *scratch):
    i = pl.program_id(2)
    q0 = i * t
    qT_s = _split_heads_t(qT_ref[0])

    def bias_of(kb, last):
        return bias_ref[(q0 - kb * t) // t]

    _skewed_flash_t(jnp.maximum(q0 - reach, 0) // t, i + 1, t, qT_s, k_ref, vT_ref, bias_of, *scratch,
                    bias_every_block=True)
    acc = scratch[3][...]
    w = HEAD_DIM + VT_PAD
    o = jnp.concatenate(
        [acc[0:HEAD_DIM, 0:t] * (1.0 / acc[HEAD_DIM:HEAD_DIM + 1, 0:t]),
         acc[w:w + HEAD_DIM, t:2 * t] * (1.0 / acc[w + HEAD_DIM:w + HEAD_DIM + 1, t:2 * t])], axis=0)
    o_ref[...] = o.T


def _dilated_attention(qT, k, vT, bsz, s_len, t):
    t_tokens = k.shape[0]
    nq = s_len // t
    reach = max(wd for wd, _ in C_PATTERNS)
    bias = jnp.swapaxes(_bias_tables((reach + t - 1) // t + 1, t, t, t, _dilated_log2_multiplicity),
                        1, 2)
    return pl.pallas_call(
        functools.partial(_dilated_kernel, t, reach),
        grid=(bsz, N_PAIRS, nq),
        in_specs=[
            pl.BlockSpec((1, LANES, t), lambda b, j, i: (j, 0, b * nq + i)),
            pl.BlockSpec((s_len, LANES), lambda b, j, i: (b, j)),
            pl.BlockSpec((1, vT.shape[1], s_len), lambda b, j, i: (j, 0, b)),
            pl.BlockSpec(bias.shape, lambda b, j, i: (0, 0, 0)),
        ],
        out_specs=pl.BlockSpec((t, LANES), lambda b, j, i: (b * nq + i, j)),
        out_shape=jax.ShapeDtypeStruct((t_tokens, GROUP), F32),
        scratch_shapes=_flash_t_scratch(t, 2 * t, vT.shape[1]),
        compiler_params=_params("parallel", "parallel", "parallel"),
        name="dilated_attention",
    )(qT, k, vT, bias)


def _diff_kernel(t, lambda_init, qT_ref, k_ref, vT_ref, bias_ref, lq1_ref, lk1_ref, lq2_ref,
                 lk2_ref, sg_ref, o_ref, *scratch):
    i = pl.program_id(2)
    qT_s = _split_heads_t(qT_ref[0])

    def bias_of(kb, last):
        return bias_ref[...] if last else None

    _skewed_flash_t(0, i + 1, t, qT_s, k_ref, vT_ref, bias_of, *scratch)
    acc = scratch[3][...]
    o = acc[0:LANES] * (1.0 / acc[LANES:LANES + 1])
    lam = (jnp.exp(jnp.sum(lq1_ref[...] * lk1_ref[...], axis=1, keepdims=True))
           - jnp.exp(jnp.sum(lq2_ref[...] * lk2_ref[...], axis=1, keepdims=True)) + lambda_init)
    o = o[:, 0:t] - lam * o[:, t:2 * t]
    ms = jnp.mean(o * o, axis=0, keepdims=True)
    o_ref[...] = ((o * lax.rsqrt(ms + NORM_EPS) * sg_ref[...]) * (1.0 - lambda_init)).T


def _diff_attention(qT, k, vT, lq1, lk1, lq2, lk2, sub_gain, lambda_init, bsz, s_len, t):
    t_tokens = k.shape[0]
    nq = s_len // t
    bias = _bias_tables(1, t, t, t, lambda d: np.where(d >= 0, 0.0, NEG))[0].T
    vec = lambda a: a.reshape(1, -1).astype(F32)
    small = lambda n: pl.BlockSpec((1, n), lambda b, j, i: (0, 0))
    return pl.pallas_call(
        functools.partial(_diff_kernel, t, lambda_init),
        grid=(bsz, N_PAIRS, nq),
        in_specs=[
            pl.BlockSpec((1, LANES, t), lambda b, j, i: (j, 0, b * nq + i)),
            pl.BlockSpec((s_len, LANES), lambda b, j, i: (b, j)),
            pl.BlockSpec((1, vT.shape[1], s_len), lambda b, j, i: (j, 0, b)),
            pl.BlockSpec((t, t), lambda b, j, i: (0, 0)),
            small(HEAD_DIM), small(HEAD_DIM), small(HEAD_DIM), small(HEAD_DIM),
            pl.BlockSpec((LANES, 1), lambda b, j, i: (0, 0)),
        ],
        out_specs=pl.BlockSpec((t, LANES), lambda b, j, i: (b * nq + i, j)),
        out_shape=jax.ShapeDtypeStruct((t_tokens, GROUP), F32),
        scratch_shapes=_flash_t_scratch(t, 2 * t, vT.shape[1]),
        compiler_params=_params("parallel", "parallel", "parallel"),
        name="diff_attention",
    )(qT, k, vT, bias, vec(lq1), vec(lk1), vec(lq2), vec(lk2), sub_gain.reshape(-1, 1).astype(F32))


def _outproj_kernel(o1_ref, o2_ref, g_ref, x_ref, p_ref, wo_ref, pg_ref, wg_ref, wp_ref, out_ref):
    g = g_ref[...]
    y1 = (o1_ref[...] * g[:, 0:GROUP]).astype(BF16)
    y2 = (o2_ref[...] * g[:, GROUP:2 * GROUP]).astype(BF16)
    x1 = (x_ref[...]
          + jnp.dot(y1, wo_ref[0:GROUP, :], preferred_element_type=F32)
          + jnp.dot(y2, wo_ref[GROUP:2 * GROUP, :], preferred_element_type=F32))
    ms = jnp.mean(x1 * x1, axis=-1, keepdims=True)
    hn = (x1 * lax.rsqrt(ms + NORM_EPS) * pg_ref[...]).astype(BF16)
    z = jnp.dot(hn, wg_ref[...], preferred_element_type=F32)
    gate = 1.0 / (1.0 + jnp.exp(-z))
    pp = jnp.dot(p_ref[...].astype(BF16), wp_ref[...], preferred_element_type=F32)
    out_ref[...] = x1 + pp * gate


def _outproj(o1, o2, gate, x2, p2, w_out, ple_gain, w_gate, w_proj, tm):
    t_tokens = x2.shape[0]
    tok = lambda n: pl.BlockSpec((tm, n), lambda i: (i, 0))
    full = lambda a: pl.BlockSpec(a.shape, lambda i: (0, 0))
    pg = ple_gain.reshape(1, D_MODEL)
    return pl.pallas_call(
        _outproj_kernel,
        grid=(t_tokens // tm,),
        in_specs=[tok(GROUP), tok(GROUP), tok(2 * GROUP), tok(D_MODEL), tok(PLE_DIM),
                  full(w_out), full(pg), full(w_gate), full(w_proj)],
        out_specs=tok(D_MODEL),
        out_shape=jax.ShapeDtypeStruct((t_tokens, D_MODEL), F32),
        compiler_params=_params("parallel"),
        name="outproj_ple",
    )(o1, o2, gate, x2, p2, w_out, pg, w_gate, w_proj)


TM = 256
TQ_DSA = 128
TK_DSA = 512
TQ_WINDOW = 256
T_DILATED = 512
T_DIFF = 512


def _rope_tables(s_len):
    half = HEAD_DIM // 2
    inv = ROPE_THETA ** (-jnp.arange(half, dtype=F32) / half)
    ang = jnp.arange(s_len).astype(F32)[:, None] * inv[None, :]
    cos = jnp.tile(jnp.cos(ang), (1, 4))
    sin = jnp.sin(ang)
    sin_signed = jnp.tile(jnp.concatenate([-sin, sin], axis=1), (1, 2))
    return cos, sin_signed


def _pair_gain(g):
    return jnp.concatenate([g, g]).astype(F32)


def _even_layer(x2, p2, bsz, s_len, tables, norm_gain, w_in, w_out, a_q_gain, a_k_gain, idx_k_gain,
                b_q_gain, b_k_gain, b_sinks, ple_gain, w_gate, w_proj):
    hd = HEAD_DIM
    offs = np.cumsum([0, 512, hd, hd, 512, hd, 8, 512, 512, 2 * hd, 2 * hd, 512])
    aq, ak, av, iq, ik, iw, ag, bq, bk, bv, bg = [w_in[:, offs[n]:offs[n + 1]] for n in range(11)]
    dup = lambda c: jnp.concatenate([c, c], axis=1)
    w = jnp.concatenate(
        [aq, iq, bq, ag, bg,
         dup(ak), dup(ik), dup(bk[:, :hd]), dup(bk[:, hd:]),
         dup(av), dup(bv[:, :hd]), dup(bv[:, hd:]),
         jnp.pad(iw, ((0, 0), (0, LANES - 8)))], axis=1).astype(BF16)
    ones = jnp.ones((LANES,), F32)
    gains = jnp.stack(
        [_pair_gain(a_q_gain)] * 4 + [ones] * 4 + [_pair_gain(b_q_gain)] * 4 + [ones] * 8
        + [_pair_gain(a_k_gain), _pair_gain(idx_k_gain), _pair_gain(b_k_gain), _pair_gain(b_k_gain)]
        + [ones] * 4)
    idx_scale = (8 * hd) ** -0.5
    plan = ([("k", 0, r, True, QSCALE) for r in range(4)]
            + [("k", 1, r, False, 1.0) for r in range(4)]
            + [("q", 2, c, True, QSCALE) for c in range(4)]
            + [("silu", 3, c) for c in range(8)]
            + [("q", 4, 0, True, 1.0)]
            + [("q", 5, 0, True, 1.0)]
            + [("k", 6, r, True, 1.0) for r in range(2)]
            + [("vt", 7, 0, HEAD_DIM, 1)]
            + [("v2", 8, 2 * r) for r in range(2)]
            + [("iw", 9, idx_scale)])
    out_defs = [("kt", 4, BF16), ("kt", 4, BF16), ("tok", 512, BF16), ("tok", 1024, F32),
                ("tok", LANES, BF16), ("tok", LANES, BF16), ("kt", 2, BF16),
                ("vt", (1, HEAD_DIM + VT_PAD), BF16), ("v", 4, BF16), ("rows", 8, F32)]
    qaT, qiT, qb, gate, ka, ki, kbT, vaT, vb, iwT = _inproj(x2, norm_gain, w, gains, *tables, plan,
                                                            out_defs, TM)
    oa = _dsa_attention(qiT, iwT, ki, qaT, ka, vaT, bsz, s_len, min(TQ_DSA, s_len), TK_DSA)
    ob = _window_attention(qb, kbT, vb, b_sinks.astype(F32), bsz, s_len, TQ_WINDOW)
    return _outproj(oa, ob, gate, x2, p2, w_out.astype(BF16), ple_gain, w_gate.astype(BF16),
                    w_proj.astype(BF16), TM)


def _odd_layer(x2, p2, bsz, s_len, tables, norm_gain, w_in, w_out, c_q_gain, c_k_gain, d_q_gain,
               d_k_gain, lq1, lk1, lq2, lk2, sub_gain, lambda_init, ple_gain, w_gate, w_proj):
    ones = jnp.ones((LANES,), F32)
    gains = jnp.stack([_pair_gain(c_q_gain)] * 4 + [_pair_gain(c_k_gain)] * 4 + [ones] * 8
                      + [_pair_gain(d_q_gain)] * 4 + [_pair_gain(d_k_gain)] * 4 + [ones] * 8)
    plan = ([("k", 0, r, True, QSCALE) for r in range(4)]
            + [("q", 1, c, True, 1.0) for c in range(4)]
            + [("vt", 2, r, HEAD_DIM, 2) for r in range(4)]
            + [("silu", 3, c) for c in range(4)]
            + [("k", 4, r, True, QSCALE) for r in range(4)]
            + [("q", 5, c, True, 1.0) for c in range(4)]
            + [("vt", 6, r, 2 * HEAD_DIM, 1) for r in range(4)]
            + [("silu", 3, 4 + c) for c in range(4)])
    out_defs = [("kt", 4, BF16), ("tok", 512, BF16), ("vt", (4, 2 * (HEAD_DIM + VT_PAD)), BF16),
                ("tok", 1024, F32),
                ("kt", 4, BF16), ("tok", 512, BF16), ("vt", (4, 2 * HEAD_DIM + VT_PAD), BF16)]
    qcT, kc, vcT, gate, qdT, kd, vdT = _inproj(x2, norm_gain, w_in.astype(BF16), gains, *tables,
                                               plan, out_defs, TM)
    oc = _dilated_attention(qcT, kc, vcT, bsz, s_len, T_DILATED)
    od = _diff_attention(qdT, kd, vdT, lq1, lk1, lq2, lk2, sub_gain, lambda_init, bsz, s_len, T_DIFF)
    return _outproj(oc, od, gate, x2, p2, w_out.astype(BF16), ple_gain, w_gate.astype(BF16),
                    w_proj.astype(BF16), TM)


def kernel(x, p, norm_gain, w_in_even, w_out_even, a_q_gain, a_k_gain, idx_k_gain, b_q_gain, b_k_gain, b_sinks, w_in_odd, w_out_odd, c_q_gain, c_k_gain, d_q_gain, d_k_gain, d_lambda_q1, d_lambda_k1, d_lambda_q2, d_lambda_k2, d_subln_gain, ple_norm_gain, w_ple_gate, w_ple_proj):
    bsz, s_len, d_model = x.shape
    depth = p.shape[0]
    assert d_model == D_MODEL and s_len % 512 == 0
    tables = _rope_tables(s_len)
    x2 = x.reshape(bsz * s_len, d_model)
    for i in range(depth):
        j = i // 2
        p2 = p[i].reshape(bsz * s_len, PLE_DIM)
        if i % 2 == 0:
            x2 = _even_layer(x2, p2, bsz, s_len, tables, norm_gain[i], w_in_even[j], w_out_even[j],
                             a_q_gain[j], a_k_gain[j], idx_k_gain[j], b_q_gain[j], b_k_gain[j],
                             b_sinks[j], ple_norm_gain[i], w_ple_gate[i], w_ple_proj[i])
        else:
            lambda_init = 0.8 - 0.6 * math.exp(-0.3 * i)
            x2 = _odd_layer(x2, p2, bsz, s_len, tables, norm_gain[i], w_in_odd[j], w_out_odd[j],
                            c_q_gain[j], c_k_gain[j], d_q_gain[j], d_k_gain[j], d_lambda_q1[j],
                            d_lambda_k1[j], d_lambda_q2[j], d_lambda_k2[j], d_subln_gain[j],
                            lambda_init, ple_norm_gain[i], w_ple_gate[i], w_ple_proj[i])
    return x2.reshape(bsz, s_len, d_model)
```

```python
import functools
import math

import numpy as np
import jax
import jax.numpy as jnp
from jax import lax
from jax.experimental import pallas as pl
from jax.experimental.pallas import tpu as pltpu

F32 = jnp.float32
BF16 = jnp.bfloat16
I32 = jnp.int32

D_MODEL = 1024
HEAD_DIM = 64
ROPE_THETA = 10000.0
NORM_EPS = 1e-6
PLE_DIM = 256
TOPK_MAX = 256
B_WINDOW = 128
C_PATTERNS = ((128, 1), (512, 4), (2048, 16))
N_PAIRS = 4
LANES = 128
GROUP = 512
VT_PAD = 16
LOG2E = 1.4426950408889634
QSCALE = HEAD_DIM ** -0.5 * LOG2E
NEG = -1e30
F32_MAX = float(np.finfo(np.float32).max)
FIXED_PROBES = 16
MAX_PROBES = 20
VMEM_LIMIT = 48 * 1024 * 1024


def _params(*sem):
    return pltpu.CompilerParams(dimension_semantics=sem, vmem_limit_bytes=VMEM_LIMIT)


def _norm_rope(y, gain, cos, sin_signed, bd, first_half, norm):
    if norm:
        ss = y * y
        hi = ss.astype(BF16)
        lo = (ss - hi.astype(F32)).astype(BF16)
        seg = (jnp.dot(hi, bd, preferred_element_type=F32)
               + jnp.dot(lo, bd, preferred_element_type=F32))
        y = y * lax.rsqrt(seg * (1.0 / HEAD_DIM) + NORM_EPS) * gain
    rot = jnp.where(first_half, pltpu.roll(y, 96, 1), pltpu.roll(y, 32, 1))
    return y * cos + rot * sin_signed


def _inproj_kernel(plan, x_ref, ng_ref, w_ref, cos_ref, sin_ref, gains_ref, bd_ref, *outs):
    x = x_ref[...]
    ms = jnp.mean(x * x, axis=-1, keepdims=True)
    h = (x * lax.rsqrt(ms + NORM_EPS) * ng_ref[...]).astype(BF16)
    cos = cos_ref[...]
    sin = sin_ref[...]
    bd = bd_ref[...]
    tm = x.shape[0]
    lane = lax.broadcasted_iota(I32, (tm, LANES), 1)
    lo_half = lane < HEAD_DIM
    first_half = (lane & (HEAD_DIM - 1)) < HEAD_DIM // 2
    one_at_64 = jnp.where(lane == HEAD_DIM, 1.0, 0.0)
    one_at_0 = jnp.where(lane == 0, 1.0, 0.0)
    for g in range(len(plan) // 4):
        yg = jnp.dot(h, w_ref[:, g * GROUP:(g + 1) * GROUP], preferred_element_type=F32)
        for c4 in range(4):
            c = g * 4 + c4
            y = yg[:, c4 * LANES:(c4 + 1) * LANES]
            step = plan[c]
            kind, out = step[0], outs[step[1]]
            if kind == "q":
                _, _, col, norm, mult = step
                y = _norm_rope(y, gains_ref[c:c + 1, :], cos, sin, bd, first_half, norm)
                out[:, col * LANES:(col + 1) * LANES] = (y * mult).astype(BF16)
            elif kind == "k":
                _, _, row, norm, mult = step
                y = _norm_rope(y, gains_ref[c:c + 1, :], cos, sin, bd, first_half, norm)
                out[row] = (y * mult).T.astype(BF16)
            elif kind == "vt":
                _, _, row, head_dim, n_heads = step
                yT = y.T.astype(BF16)
                sub = lax.broadcasted_iota(I32, (VT_PAD, tm), 0)
                ones_row = jnp.where(sub == 0, 1.0, 0.0).astype(BF16)
                for n in range(n_heads):
                    base = n * (head_dim + VT_PAD)
                    out[row, base:base + head_dim, :] = yT[n * head_dim:(n + 1) * head_dim]
                    out[row, base + head_dim:base + head_dim + VT_PAD, :] = ones_row
            elif kind == "silu":
                _, _, col = step
                out[:, col * LANES:(col + 1) * LANES] = y * (1.0 / (1.0 + jnp.exp(-y)))
            elif kind == "v2":
                _, _, row = step
                out[row] = jnp.where(lo_half, y, one_at_64).astype(BF16)
                out[row + 1] = jnp.where(lo_half, one_at_0, y).astype(BF16)
            elif kind == "v1":
                _, _, row = step
                out[row, :, 0:LANES] = y.astype(BF16)
                out[row, :, LANES:2 * LANES] = one_at_0.astype(BF16)
            elif kind == "iw":
                _, _, mult = step
                out[...] = (y * mult).T[0:8]
            else:
                raise ValueError(kind)


def _inproj(x2, norm_gain, w, gains, cos_t, sin_t, plan, out_defs, tm):
    t_tokens = x2.shape[0]
    s_len = cos_t.shape[0]
    n_cols = w.shape[1]
    n_tiles = t_tokens // tm
    s_tiles = s_len // tm
    bd = jnp.asarray(np.kron(np.eye(2), np.ones((HEAD_DIM, HEAD_DIM))), BF16)
    out_shapes, out_specs = [], []
    for kind, n, dtype in out_defs:
        if kind == "tok":
            out_shapes.append(jax.ShapeDtypeStruct((t_tokens, n), dtype))
            out_specs.append(pl.BlockSpec((tm, n), lambda i: (i, 0)))
        elif kind == "kt":
            out_shapes.append(jax.ShapeDtypeStruct((n, LANES, t_tokens), dtype))
            out_specs.append(pl.BlockSpec((n, LANES, tm), lambda i: (0, 0, i)))
        elif kind == "v":
            out_shapes.append(jax.ShapeDtypeStruct((n, t_tokens, LANES), dtype))
            out_specs.append(pl.BlockSpec((n, tm, LANES), lambda i: (0, i, 0)))
        elif kind == "vt":
            out_shapes.append(jax.ShapeDtypeStruct((n[0], n[1], t_tokens), dtype))
            out_specs.append(pl.BlockSpec((n[0], n[1], tm), lambda i: (0, 0, i)))
        elif kind == "rows":
            out_shapes.append(jax.ShapeDtypeStruct((n, t_tokens), dtype))
            out_specs.append(pl.BlockSpec((n, tm), lambda i: (0, i)))
        elif kind == "vw":
            out_shapes.append(jax.ShapeDtypeStruct((n, t_tokens, 2 * LANES), dtype))
            out_specs.append(pl.BlockSpec((n, tm, 2 * LANES), lambda i: (0, i, 0)))
    return pl.pallas_call(
        functools.partial(_inproj_kernel, plan),
        grid=(n_tiles,),
        in_specs=[
            pl.BlockSpec((tm, D_MODEL), lambda i: (i, 0)),
            pl.BlockSpec((1, D_MODEL), lambda i: (0, 0)),
            pl.BlockSpec((D_MODEL, n_cols), lambda i: (0, 0)),
            pl.BlockSpec((tm, LANES), lambda i: (i % s_tiles, 0)),
            pl.BlockSpec((tm, LANES), lambda i: (i % s_tiles, 0)),
            pl.BlockSpec(gains.shape, lambda i: (0, 0)),
            pl.BlockSpec((LANES, LANES), lambda i: (0, 0)),
        ],
        out_specs=out_specs,
        out_shape=out_shapes,
        compiler_params=_params("parallel"),
        name="inproj",
    )(x2, norm_gain.reshape(1, D_MODEL), w, cos_t, sin_t, gains, bd)


def _split_heads(qc):
    lane = lax.broadcasted_iota(I32, qc.shape, 1)
    zero = jnp.zeros_like(qc)
    return jnp.where(lane < HEAD_DIM, qc, zero), jnp.where(lane >= HEAD_DIM, qc, zero)


def _stack_heads(q_ref):
    tq, width = q_ref.shape
    parts = []
    for j in range(width // LANES):
        parts.extend(_split_heads(q_ref[:, j * LANES:(j + 1) * LANES]))
    return jnp.concatenate(parts, axis=0)


def _merge_heads(acc_lo, acc_hi):
    lane = lax.broadcasted_iota(I32, acc_lo.shape, 1)
    o_lo = acc_lo * (1.0 / acc_lo[:, HEAD_DIM:HEAD_DIM + 1])
    o_hi = acc_hi * (1.0 / acc_hi[:, 0:1])
    return jnp.where(lane < HEAD_DIM, o_lo, o_hi)


def _dsa_kernel(tq, tk, topk, s_len, qiT_ref, iwT_ref, ki_ref, qaT_ref, ka_ref, vaT_ref, o_ref,
                sc_ref, *flash_scratch):
    i = pl.program_id(1)
    q0 = i * tq
    nkb = (q0 + tq + tk - 1) // tk
    n_heads = 2 * N_PAIRS
    key_iota = lax.broadcasted_iota(I32, (tk, tq), 0)
    q_pos = q0 + lax.broadcasted_iota(I32, (tk, tq), 1)

    w = iwT_ref[...]
    qT_idx = jnp.concatenate([_split_heads_t(qiT_ref[j]) for j in range(N_PAIRS)], axis=1)

    def dots_into(dst, kb):
        dst[...] = jnp.dot(ki_ref[pl.ds(pl.multiple_of(kb * tk, tk), tk), :], qT_idx,
                           preferred_element_type=F32)

    def score_from(src, kb, last=False):
        k0 = pl.multiple_of(kb * tk, tk)
        score = jnp.zeros((tk, tq), F32)
        for h in range(n_heads):
            score = score + w[h:h + 1, :] * jnp.maximum(src[:, h * tq:(h + 1) * tq], 0.0)
        sc_ref[pl.ds(k0, tk), :] = jnp.where(key_iota + k0 <= q_pos, score, -jnp.inf)

    _skewed_blocks(0, nkb, dots_into, score_from, flash_scratch[0], flash_scratch[1])

    n_acc = 64

    def fold_keys(step, init, reduce):
        def blk(kb, a):
            k0 = pl.multiple_of(kb * tk, tk)
            return step(a, sc_ref[pl.ds(k0, tk), :].reshape(tk // n_acc, n_acc, tq), k0)
        a = lax.fori_loop(0, nkb, blk, jnp.full((n_acc, tq), init, F32))
        return reduce(a, axis=0, keepdims=True)

    def count(pred):
        return fold_keys(lambda a, x, k0: a + jnp.sum(jnp.where(pred(x, k0), 1.0, 0.0), axis=0),
                         0.0, jnp.sum)

    keep_all = (q_pos[0:1, :] + 1).astype(F32) <= topk
    row_max = fold_keys(lambda a, x, k0: jnp.maximum(a, jnp.max(x, axis=0)), -jnp.inf, jnp.max)
    row_min = fold_keys(
        lambda a, x, k0: jnp.minimum(a, jnp.min(jnp.where(x == -jnp.inf, jnp.inf, x), axis=0)),
        jnp.inf, jnp.min)

    RUN, DONE = 0.0, 1.0

    def probe(_, carry):
        lo, hi, state, thr = carry
        cand = lo + (hi - lo) * 0.5
        collapsed = (cand <= lo) | (cand >= hi)
        tot = count(lambda x, c0: x >= cand)
        running = state == RUN
        hit = running & (tot == topk)
        state = jnp.where(hit, DONE, jnp.where(running & collapsed, 2.0, state))
        thr = jnp.where(hit, cand, thr)
        return jnp.where(tot > topk, cand, lo), jnp.where(tot < topk, cand, hi), state, thr

    def n_running(state):
        return jnp.sum(jnp.where(state == RUN, 1.0, 0.0))

    def more_probes(carry):
        _, it, rest = carry
        rest = probe(0, probe(0, rest))
        return jnp.where(it + 2 >= MAX_PROBES, 0.0, n_running(rest[2])), it + 2, rest

    state0 = jnp.where(keep_all, DONE, RUN)
    hi0 = row_max + (jnp.abs(row_max) * 2.0 ** -20 + 1e-30)
    rest = lax.fori_loop(0, FIXED_PROBES, probe,
                         (row_min, hi0, state0, jnp.full((1, tq), -F32_MAX, F32)))
    _, _, (_, hi, state, thr) = lax.while_loop(
        lambda c: c[0] > 0.0, more_probes, (n_running(rest[2]), jnp.int32(FIXED_PROBES), rest))
    n_unresolved = jnp.sum(jnp.where(state == DONE, 0.0, 1.0))

    @pl.when(n_unresolved == 0.0)
    def _():
        def bias_block(kb, carry):
            k0 = pl.multiple_of(kb * tk, tk)
            sc_ref[pl.ds(k0, tk), :] = jnp.where(sc_ref[pl.ds(k0, tk), :] >= thr, 0.0, NEG)
            return carry
        lax.fori_loop(0, nkb, bias_block, 0)

    @pl.when(n_unresolved > 0.0)
    def _():
        def step_down(carry):
            _, hi, kth, found = carry
            below = fold_keys(
                lambda a, x, k0: jnp.maximum(a, jnp.max(jnp.where(x < hi, x, -jnp.inf), axis=0)),
                -jnp.inf, jnp.max)
            tot = count(lambda x, c0: x >= below)
            newly = (found == 0.0) & (tot >= topk)
            kth = jnp.where(newly, below, kth)
            found = jnp.where(newly, 1.0, found)
            return jnp.sum(1.0 - found), jnp.where(found == 0.0, below, hi), kth, found

        found0 = jnp.where(state == DONE, 1.0, 0.0)
        _, _, kth, _ = lax.while_loop(lambda c: c[0] > 0.0, step_down,
                                      (jnp.sum(1.0 - found0), hi, thr, found0))
        need = topk - count(lambda x, c0: x > kth)
        tri = jnp.where(lax.broadcasted_iota(I32, (tk, tk), 1) <= lax.broadcasted_iota(I32, (tk, tk), 0),
                        1.0, 0.0).astype(BF16)

        def bias_block(kb, ties_before):
            k0 = pl.multiple_of(kb * tk, tk)
            x = sc_ref[pl.ds(k0, tk), :]
            is_tie = x == kth
            rank = ties_before + jnp.dot(tri, jnp.where(is_tie, 1.0, 0.0).astype(BF16),
                                         preferred_element_type=F32)
            tie_bias = jnp.where(rank <= need, 0.0, NEG)
            sc_ref[pl.ds(k0, tk), :] = jnp.where(x > kth, 0.0, jnp.where(is_tie, tie_bias, NEG))
            return rank[tk - 1:tk, :]
        lax.fori_loop(0, nkb, bias_block, jnp.zeros((1, tq), F32))

    qT_s = jnp.concatenate([_split_heads_t(qaT_ref[j]) for j in range(N_PAIRS)], axis=1)

    def bias_of(kb, last):
        return sc_ref[pl.ds(pl.multiple_of(kb * tk, tk), tk), :]

    _skewed_flash_t(0, nkb, tk, qT_s, ka_ref, vaT_ref, bias_of, *flash_scratch,
                    bias_every_block=True)
    acc = flash_scratch[3][...]
    o = acc[0:HEAD_DIM] * (1.0 / acc[HEAD_DIM:HEAD_DIM + 1])
    for j in range(N_PAIRS):
        pair = jnp.concatenate([o[:, 2 * j * tq:(2 * j + 1) * tq],
                                o[:, (2 * j + 1) * tq:(2 * j + 2) * tq]], axis=0)
        o_ref[:, j * LANES:(j + 1) * LANES] = pair.T


def _dsa_attention(qiT, iwT, ki, qaT, ka, vaT, bsz, s_len, tq, tk):
    t_tokens = ki.shape[0]
    nq = s_len // tq
    topk = min(TOPK_MAX, s_len // 4)
    q_spec = pl.BlockSpec((N_PAIRS, LANES, tq), lambda b, i: (0, 0, b * nq + i))
    k_spec = pl.BlockSpec((s_len, LANES), lambda b, i: (b, 0))
    return pl.pallas_call(
        functools.partial(_dsa_kernel, tq, tk, float(topk), s_len),
        grid=(bsz, nq),
        in_specs=[
            q_spec,
            pl.BlockSpec((2 * N_PAIRS, tq), lambda b, i: (0, b * nq + i)),
            k_spec,
            q_spec,
            k_spec,
            pl.BlockSpec((1, vaT.shape[1], s_len), lambda b, i: (0, 0, b)),
        ],
        out_specs=pl.BlockSpec((tq, GROUP), lambda b, i: (b * nq + i, 0)),
        out_shape=jax.ShapeDtypeStruct((t_tokens, GROUP), F32),
        scratch_shapes=[pltpu.VMEM((s_len, tq), F32)]
        + _flash_t_scratch(tk, 2 * N_PAIRS * tq, vaT.shape[1]),
        compiler_params=_params("parallel", "parallel"),
        name="dsa_attention",
    )(qiT, iwT, ki, qaT, ka, vaT)


def _bias_tables(n, step, tq, tk, fn, keys_first=False):
    d = np.arange(n)[:, None, None] * step + np.arange(tq)[None, :, None] - np.arange(tk)[None, None, :]
    table = np.asarray(fn(d), np.float32)
    return jnp.asarray(np.ascontiguousarray(np.swapaxes(table, 1, 2)) if keys_first else table)


def _window_kernel(tq, q_ref, kT_ref, v_ref, bias_ref, sink_ref, o_ref):
    j = pl.program_id(1)
    i = pl.program_id(2)
    q0 = i * tq
    wk = tq + B_WINDOW
    k0 = pl.multiple_of(jnp.maximum(q0 - B_WINDOW, 0), LANES)
    kT = kT_ref[0, :, pl.ds(k0, wk)]
    bias = bias_ref[(q0 - k0) // B_WINDOW]
    lane = lax.broadcasted_iota(I32, (tq, LANES), 1)
    s_both = jnp.dot(_stack_heads(q_ref), kT, preferred_element_type=F32)
    accs = []
    for half in range(2):
        sink = sink_ref[2 * j + half] * LOG2E
        s = s_both[half * tq:(half + 1) * tq] + bias
        m = jnp.maximum(jnp.max(s, axis=1, keepdims=True), sink)
        p = jnp.exp2(s - m).astype(BF16)
        acc = jnp.dot(p, v_ref[half, pl.ds(k0, wk), :], preferred_element_type=F32)
        accs.append(acc + jnp.where(lane == (HEAD_DIM if half == 0 else 0), jnp.exp2(sink - m), 0.0))
    o_ref[...] = _merge_heads(*accs)


def _window_attention(q, kT, v, sinks, bsz, s_len, tq):
    t_tokens = q.shape[0]
    nq = s_len // tq
    bias = _bias_tables(2, B_WINDOW, tq, tq + B_WINDOW,
                        lambda d: np.where((d >= 0) & (d < B_WINDOW), 0.0, NEG))
    return pl.pallas_call(
        functools.partial(_window_kernel, tq),
        grid=(bsz, N_PAIRS, nq),
        in_specs=[
            pl.BlockSpec((tq, LANES), lambda b, j, i: (b * nq + i, j)),
            pl.BlockSpec((1, LANES, s_len), lambda b, j, i: (j // 2, 0, b)),
            pl.BlockSpec((2, s_len, LANES), lambda b, j, i: (j // 2, b, 0)),
            pl.BlockSpec(bias.shape, lambda b, j, i: (0, 0, 0)),
            pl.BlockSpec(memory_space=pltpu.SMEM),
        ],
        out_specs=pl.BlockSpec((tq, LANES), lambda b, j, i: (b * nq + i, j)),
        out_shape=jax.ShapeDtypeStruct((t_tokens, GROUP), F32),
        compiler_params=_params("parallel", "parallel", "parallel"),
        name="window_attention",
    )(q, kT, v, bias, sinks)


def _dilated_log2_multiplicity(d):
    count = np.zeros(d.shape)
    for window, dilation in C_PATTERNS:
        count += (d >= 0) & (d <= window) & (d % dilation == 0)
    return np.where(count > 0, np.log2(np.maximum(count, 1)), NEG)


def _split_heads_t(qT):
    row = lax.broadcasted_iota(I32, qT.shape, 0)
    zero = jnp.zeros_like(qT)
    return jnp.concatenate([jnp.where(row < HEAD_DIM, qT, zero), jnp.where(row >= HEAD_DIM, qT, zero)],
                           axis=1)


def _skewed_blocks(lo, hi, produce_into, consume_from, sa_ref, sb_ref):
    n = hi - lo
    produce_into(sa_ref, lo)

    def pair(pp, carry):
        kb = lo + 2 * pp
        produce_into(sb_ref, kb + 1)
        consume_from(sa_ref, kb)
        produce_into(sa_ref, kb + 2)
        consume_from(sb_ref, kb + 1)
        return carry

    lax.fori_loop(0, (n - 1) // 2, pair, 0)

    @pl.when(n % 2 == 1)
    def _():
        consume_from(sa_ref, hi - 1, last=True)

    @pl.when(n % 2 == 0)
    def _():
        produce_into(sb_ref, hi - 1)
        consume_from(sa_ref, hi - 2)
        consume_from(sb_ref, hi - 1, last=True)


def _skewed_flash_t(lo, hi, tk, qT_s, k_ref, vT_ref, bias_of, sa_ref, sb_ref, m_ref, acc_ref,
                    bias_every_block=False):
    m_ref[...] = jnp.full(m_ref.shape, NEG, F32)
    acc_ref[...] = jnp.zeros(acc_ref.shape, F32)

    def add_bias(s, bias):
        if bias is None:
            return s
        w = bias.shape[1]
        return jnp.concatenate([s[:, r * w:(r + 1) * w] + bias for r in range(s.shape[1] // w)],
                               axis=1)

    def logits_into(dst, kb):
        s = jnp.dot(k_ref[pl.ds(pl.multiple_of(kb * tk, tk), tk), :], qT_s,
                    preferred_element_type=F32)
        dst[...] = add_bias(s, bias_of(kb, False)) if bias_every_block else s

    def update_from(src, kb, last=False):
        s = src[...]
        if not bias_every_block:
            s = add_bias(s, bias_of(kb, last))
        m = m_ref[...]
        m_new = jnp.maximum(m, jnp.max(s, axis=0, keepdims=True))
        p = jnp.exp2(s - m_new).astype(BF16)
        vT = vT_ref[0, :, pl.ds(pl.multiple_of(kb * tk, tk), tk)]
        acc_ref[...] = jnp.exp2(m - m_new) * acc_ref[...] + jnp.dot(vT, p, preferred_element_type=F32)
        m_ref[...] = m_new

    _skewed_blocks(lo, hi, logits_into, update_from, sa_ref, sb_ref)


def _flash_t_scratch(tk, n_queries, n_value_rows):
    return [pltpu.VMEM((tk, n_queries), F32), pltpu.VMEM((tk, n_queries), F32),
            pltpu.VMEM((1, n_queries), F32), pltpu.VMEM((n_value_rows, n_queries), F32)]


def _dilated_kernel(t, reach, qT_ref, k_ref, vT_ref, bias_ref, o_ref, *scratch):
    i = pl.program_id(2)
    q0 = i * t
    qT_s = _split_heads_t(qT_ref[0])

    def bias_of(kb, last):
        return bias_ref[(q0 - kb * t) // t]

    _skewed_flash_t(jnp.maximum(q0 - reach, 0) // t, i + 1, t, qT_s, k_ref, vT_ref, bias_of, *scratch,
                    bias_every_block=True)
    acc = scratch[3][...]
    w = HEAD_DIM + VT_PAD
    o = jnp.concatenate(
        [acc[0:HEAD_DIM, 0:t] * (1.0 / acc[HEAD_DIM:HEAD_DIM + 1, 0:t]),
         acc[w:w + HEAD_DIM, t:2 * t] * (1.0 / acc[w + HEAD_DIM:w + HEAD_DIM + 1, t:2 * t])], axis=0)
    o_ref[...] = o.T


def _dilated_attention(qT, k, vT, bsz, s_len, t):
    t_tokens = k.shape[0]
    nq = s_len // t
    reach = max(wd for wd, _ in C_PATTERNS)
    bias = _bias_tables((reach + t - 1) // t + 1, t, t, t, _dilated_log2_multiplicity,
                        keys_first=True)
    return pl.pallas_call(
        functools.partial(_dilated_kernel, t, reach),
        grid=(bsz, N_PAIRS, nq),
        in_specs=[
            pl.BlockSpec((1, LANES, t), lambda b, j, i: (j, 0, b * nq + i)),
            pl.BlockSpec((s_len, LANES), lambda b, j, i: (b, j)),
            pl.BlockSpec((1, vT.shape[1], s_len), lambda b, j, i: (j, 0, b)),
            pl.BlockSpec(bias.shape, lambda b, j, i: (0, 0, 0)),
        ],
        out_specs=pl.BlockSpec((t, LANES), lambda b, j, i: (b * nq + i, j)),
        out_shape=jax.ShapeDtypeStruct((t_tokens, GROUP), F32),
        scratch_shapes=_flash_t_scratch(t, 2 * t, vT.shape[1]),
        compiler_params=_params("parallel", "parallel", "parallel"),
        name="dilated_attention",
    )(qT, k, vT, bias)


def _diff_kernel(t, lambda_init, qT_ref, k_ref, vT_ref, bias_ref, lq1_ref, lk1_ref, lq2_ref,
                 lk2_ref, sg_ref, o_ref, *scratch):
    i = pl.program_id(2)
    qT_s = _split_heads_t(qT_ref[0])

    def bias_of(kb, last):
        return bias_ref[...] if last else None

    _skewed_flash_t(0, i + 1, t, qT_s, k_ref, vT_ref, bias_of, *scratch)
    acc = scratch[3][...]
    o = acc[0:LANES] * (1.0 / acc[LANES:LANES + 1])
    lam = (jnp.exp(jnp.sum(lq1_ref[...] * lk1_ref[...], axis=1, keepdims=True))
           - jnp.exp(jnp.sum(lq2_ref[...] * lk2_ref[...], axis=1, keepdims=True)) + lambda_init)
    o = o[:, 0:t] - lam * o[:, t:2 * t]
    ms = jnp.mean(o * o, axis=0, keepdims=True)
    o_ref[...] = ((o * lax.rsqrt(ms + NORM_EPS) * sg_ref[...]) * (1.0 - lambda_init)).T


def _diff_attention(qT, k, vT, lq1, lk1, lq2, lk2, sub_gain, lambda_init, bsz, s_len, t):
    t_tokens = k.shape[0]
    nq = s_len // t
    bias = _bias_tables(1, t, t, t, lambda d: np.where(d >= 0, 0.0, NEG), keys_first=True)[0]
    vec = lambda a: a.reshape(1, -1).astype(F32)
    small = lambda n: pl.BlockSpec((1, n), lambda b, j, i: (0, 0))
    return pl.pallas_call(
        functools.partial(_diff_kernel, t, lambda_init),
        grid=(bsz, N_PAIRS, nq),
        in_specs=[
            pl.BlockSpec((1, LANES, t), lambda b, j, i: (j, 0, b * nq + i)),
            pl.BlockSpec((s_len, LANES), lambda b, j, i: (b, j)),
            pl.BlockSpec((1, vT.shape[1], s_len), lambda b, j, i: (j, 0, b)),
            pl.BlockSpec((t, t), lambda b, j, i: (0, 0)),
            small(HEAD_DIM), small(HEAD_DIM), small(HEAD_DIM), small(HEAD_DIM),
            pl.BlockSpec((LANES, 1), lambda b, j, i: (0, 0)),
        ],
        out_specs=pl.BlockSpec((t, LANES), lambda b, j, i: (b * nq + i, j)),
        out_shape=jax.ShapeDtypeStruct((t_tokens, GROUP), F32),
        scratch_shapes=_flash_t_scratch(t, 2 * t, vT.shape[1]),
        compiler_params=_params("parallel", "parallel", "parallel"),
        name="diff_attention",
    )(qT, k, vT, bias, vec(lq1), vec(lk1), vec(lq2), vec(lk2), sub_gain.reshape(-1, 1).astype(F32))


def _outproj_kernel(o1_ref, o2_ref, g_ref, x_ref, p_ref, wo_ref, pg_ref, wg_ref, wp_ref, out_ref):
    g = g_ref[...]
    y1 = (o1_ref[...] * g[:, 0:GROUP]).astype(BF16)
    y2 = (o2_ref[...] * g[:, GROUP:2 * GROUP]).astype(BF16)
    x1 = (x_ref[...]
          + jnp.dot(y1, wo_ref[0:GROUP, :], preferred_element_type=F32)
          + jnp.dot(y2, wo_ref[GROUP:2 * GROUP, :], preferred_element_type=F32))
    ms = jnp.mean(x1 * x1, axis=-1, keepdims=True)
    hn = (x1 * lax.rsqrt(ms + NORM_EPS) * pg_ref[...]).astype(BF16)
    z = jnp.dot(hn, wg_ref[...], preferred_element_type=F32)
    gate = 1.0 / (1.0 + jnp.exp(-z))
    pp = jnp.dot(p_ref[...].astype(BF16), wp_ref[...], preferred_element_type=F32)
    out_ref[...] = x1 + pp * gate


def _outproj(o1, o2, gate, x2, p2, w_out, ple_gain, w_gate, w_proj, tm):
    t_tokens = x2.shape[0]
    tok = lambda n: pl.BlockSpec((tm, n), lambda i: (i, 0))
    full = lambda a: pl.BlockSpec(a.shape, lambda i: (0, 0))
    pg = ple_gain.reshape(1, D_MODEL)
    return pl.pallas_call(
        _outproj_kernel,
        grid=(t_tokens // tm,),
        in_specs=[tok(GROUP), tok(GROUP), tok(2 * GROUP), tok(D_MODEL), tok(PLE_DIM),
                  full(w_out), full(pg), full(w_gate), full(w_proj)],
        out_specs=tok(D_MODEL),
        out_shape=jax.ShapeDtypeStruct((t_tokens, D_MODEL), F32),
        compiler_params=_params("parallel"),
        name="outproj_ple",
    )(o1, o2, gate, x2, p2, w_out, pg, w_gate, w_proj)


TM = 256
TQ_DSA = 128
TK_DSA = 512
TQ_WINDOW = 256
T_DILATED = 512
T_DIFF = 512


def _rope_tables(s_len):
    half = HEAD_DIM // 2
    inv = ROPE_THETA ** (-jnp.arange(half, dtype=F32) / half)
    ang = jnp.arange(s_len).astype(F32)[:, None] * inv[None, :]
    cos = jnp.tile(jnp.cos(ang), (1, 4))
    sin = jnp.sin(ang)
    sin_signed = jnp.tile(jnp.concatenate([-sin, sin], axis=1), (1, 2))
    return cos, sin_signed


def _pair_gain(g):
    return jnp.concatenate([g, g]).astype(F32)


def _even_layer(x2, p2, bsz, s_len, tables, norm_gain, w_in, w_out, a_q_gain, a_k_gain, idx_k_gain,
                b_q_gain, b_k_gain, b_sinks, ple_gain, w_gate, w_proj):
    hd = HEAD_DIM
    offs = np.cumsum([0, 512, hd, hd, 512, hd, 8, 512, 512, 2 * hd, 2 * hd, 512])
    aq, ak, av, iq, ik, iw, ag, bq, bk, bv, bg = [w_in[:, offs[n]:offs[n + 1]] for n in range(11)]
    dup = lambda c: jnp.concatenate([c, c], axis=1)
    w = jnp.concatenate(
        [aq, iq, bq, ag, bg,
         dup(ak), dup(ik), dup(bk[:, :hd]), dup(bk[:, hd:]),
         dup(av), dup(bv[:, :hd]), dup(bv[:, hd:]),
         jnp.pad(iw, ((0, 0), (0, LANES - 8)))], axis=1).astype(BF16)
    ones = jnp.ones((LANES,), F32)
    gains = jnp.stack(
        [_pair_gain(a_q_gain)] * 4 + [ones] * 4 + [_pair_gain(b_q_gain)] * 4 + [ones] * 8
        + [_pair_gain(a_k_gain), _pair_gain(idx_k_gain), _pair_gain(b_k_gain), _pair_gain(b_k_gain)]
        + [ones] * 4)
    idx_scale = (8 * hd) ** -0.5
    plan = ([("k", 0, r, True, QSCALE) for r in range(4)]
            + [("k", 1, r, False, 1.0) for r in range(4)]
            + [("q", 2, c, True, QSCALE) for c in range(4)]
            + [("silu", 3, c) for c in range(8)]
            + [("q", 4, 0, True, 1.0)]
            + [("q", 5, 0, True, 1.0)]
            + [("k", 6, r, True, 1.0) for r in range(2)]
            + [("vt", 7, 0, HEAD_DIM, 1)]
            + [("v2", 8, 2 * r) for r in range(2)]
            + [("iw", 9, idx_scale)])
    out_defs = [("kt", 4, BF16), ("kt", 4, BF16), ("tok", 512, BF16), ("tok", 1024, F32),
                ("tok", LANES, BF16), ("tok", LANES, BF16), ("kt", 2, BF16),
                ("vt", (1, HEAD_DIM + VT_PAD), BF16), ("v", 4, BF16), ("rows", 8, F32)]
    qaT, qiT, qb, gate, ka, ki, kbT, vaT, vb, iwT = _inproj(x2, norm_gain, w, gains, *tables, plan,
                                                            out_defs, TM)
    oa = _dsa_attention(qiT, iwT, ki, qaT, ka, vaT, bsz, s_len, min(TQ_DSA, s_len), TK_DSA)
    ob = _window_attention(qb, kbT, vb, b_sinks.astype(F32), bsz, s_len, TQ_WINDOW)
    return _outproj(oa, ob, gate, x2, p2, w_out.astype(BF16), ple_gain, w_gate.astype(BF16),
                    w_proj.astype(BF16), TM)


def _odd_layer(x2, p2, bsz, s_len, tables, norm_gain, w_in, w_out, c_q_gain, c_k_gain, d_q_gain,
               d_k_gain, lq1, lk1, lq2, lk2, sub_gain, lambda_init, ple_gain, w_gate, w_proj):
    ones = jnp.ones((LANES,), F32)
    gains = jnp.stack([_pair_gain(c_q_gain)] * 4 + [_pair_gain(c_k_gain)] * 4 + [ones] * 8
                      + [_pair_gain(d_q_gain)] * 4 + [_pair_gain(d_k_gain)] * 4 + [ones] * 8)
    plan = ([("k", 0, r, True, QSCALE) for r in range(4)]
            + [("q", 1, c, True, 1.0) for c in range(4)]
            + [("vt", 2, r, HEAD_DIM, 2) for r in range(4)]
            + [("silu", 3, c) for c in range(4)]
            + [("k", 4, r, True, QSCALE) for r in range(4)]
            + [("q", 5, c, True, 1.0) for c in range(4)]
            + [("vt", 6, r, 2 * HEAD_DIM, 1) for r in range(4)]
            + [("silu", 3, 4 + c) for c in range(4)])
    out_defs = [("kt", 4, BF16), ("tok", 512, BF16), ("vt", (4, 2 * (HEAD_DIM + VT_PAD)), BF16),
                ("tok", 1024, F32),
                ("kt", 4, BF16), ("tok", 512, BF16), ("vt", (4, 2 * HEAD_DIM + VT_PAD), BF16)]
    qcT, kc, vcT, gate, qdT, kd, vdT = _inproj(x2, norm_gain, w_in.astype(BF16), gains, *tables,
                                               plan, out_defs, TM)
    oc = _dilated_attention(qcT, kc, vcT, bsz, s_len, T_DILATED)
    od = _diff_attention(qdT, kd, vdT, lq1, lk1, lq2, lk2, sub_gain, lambda_init, bsz, s_len, T_DIFF)
    return _outproj(oc, od, gate, x2, p2, w_out.astype(BF16), ple_gain, w_gate.astype(BF16),
                    w_proj.astype(BF16), TM)


def kernel(x, p, norm_gain, w_in_even, w_out_even, a_q_gain, a_k_gain, idx_k_gain, b_q_gain, b_k_gain, b_sinks, w_in_odd, w_out_odd, c_q_gain, c_k_gain, d_q_gain, d_k_gain, d_lambda_q1, d_lambda_k1, d_lambda_q2, d_lambda_k2, d_subln_gain, ple_norm_gain, w_ple_gate, w_ple_proj):
    bsz, s_len, d_model = x.shape
    depth = p.shape[0]
    assert d_model == D_MODEL and s_len % 512 == 0
    tables = _rope_tables(s_len)
    x2 = x.reshape(bsz * s_len, d_model)
    for i in range(depth):
        j = i // 2
        p2 = p[i].reshape(bsz * s_len, PLE_DIM)
        if i % 2 == 0:
            x2 = _even_layer(x2, p2, bsz, s_len, tables, norm_gain[i], w_in_even[j], w_out_even[j],
                             a_q_gain[j], a_k_gain[j], idx_k_gain[j], b_q_gain[j], b_k_gain[j],
                             b_sinks[j], ple_norm_gain[i], w_ple_gate[i], w_ple_proj[i])
        else:
            lambda_init = 0.8 - 0.6 * math.exp(-0.3 * i)
            x2 = _odd_layer(x2, p2, bsz, s_len, tables, norm_gain[i], w_in_odd[j], w_out_odd[j],
                            c_q_gain[j], c_k_gain[j], d_q_gain[j], d_k_gain[j], d_lambda_q1[j],
                            d_lambda_k1[j], d_lambda_q2[j], d_lambda_k2[j], d_subln_gain[j],
                            lambda_init, ple_norm_gain[i], w_ple_gate[i], w_ple_proj[i])
    return x2.reshape(bsz, s_len, d_model)
```

```python
import functools
import math

import numpy as np
import jax
import jax.numpy as jnp
from jax import lax
from jax.experimental import pallas as pl
from jax.experimental.pallas import tpu as pltpu

F32 = jnp.float32
BF16 = jnp.bfloat16
I32 = jnp.int32

D_MODEL = 1024
HEAD_DIM = 64
ROPE_THETA = 10000.0
NORM_EPS = 1e-6
PLE_DIM = 256
TOPK_MAX = 256
B_WINDOW = 128
C_PATTERNS = ((128, 1), (512, 4), (2048, 16))
N_PAIRS = 4
LANES = 128
GROUP = 512
VT_PAD = 16
LOG2E = 1.4426950408889634
QSCALE = HEAD_DIM ** -0.5 * LOG2E
NEG = -1e30
F32_MAX = float(np.finfo(np.float32).max)
FIXED_PROBES = 16
MAX_PROBES = 20
VMEM_LIMIT = 48 * 1024 * 1024


def _params(*sem):
    return pltpu.CompilerParams(dimension_semantics=sem, vmem_limit_bytes=VMEM_LIMIT)


def _norm_rope(y, gain, cos, sin_signed, bd, first_half, norm):
    if norm:
        ss = y * y
        hi = ss.astype(BF16)
        lo = (ss - hi.astype(F32)).astype(BF16)
        seg = (jnp.dot(hi, bd, preferred_element_type=F32)
               + jnp.dot(lo, bd, preferred_element_type=F32))
        y = y * lax.rsqrt(seg * (1.0 / HEAD_DIM) + NORM_EPS) * gain
    rot = jnp.where(first_half, pltpu.roll(y, 96, 1), pltpu.roll(y, 32, 1))
    return y * cos + rot * sin_signed


def _inproj_kernel(plan, x_ref, ng_ref, w_ref, cos_ref, sin_ref, gains_ref, bd_ref, *outs):
    x = x_ref[...]
    ms = jnp.mean(x * x, axis=-1, keepdims=True)
    h = (x * lax.rsqrt(ms + NORM_EPS) * ng_ref[...]).astype(BF16)
    cos = cos_ref[...]
    sin = sin_ref[...]
    bd = bd_ref[...]
    tm = x.shape[0]
    lane = lax.broadcasted_iota(I32, (tm, LANES), 1)
    lo_half = lane < HEAD_DIM
    first_half = (lane & (HEAD_DIM - 1)) < HEAD_DIM // 2
    one_at_64 = jnp.where(lane == HEAD_DIM, 1.0, 0.0)
    one_at_0 = jnp.where(lane == 0, 1.0, 0.0)
    for g in range(len(plan) // 4):
        yg = jnp.dot(h, w_ref[:, g * GROUP:(g + 1) * GROUP], preferred_element_type=F32)
        for c4 in range(4):
            c = g * 4 + c4
            y = yg[:, c4 * LANES:(c4 + 1) * LANES]
            step = plan[c]
            kind, out = step[0], outs[step[1]]
            if kind == "q":
                _, _, col, norm, mult = step
                y = _norm_rope(y, gains_ref[c:c + 1, :], cos, sin, bd, first_half, norm)
                out[:, col * LANES:(col + 1) * LANES] = (y * mult).astype(BF16)
            elif kind == "k":
                _, _, row, norm, mult = step
                y = _norm_rope(y, gains_ref[c:c + 1, :], cos, sin, bd, first_half, norm)
                out[row] = (y * mult).T.astype(BF16)
            elif kind == "vt":
                _, _, row, head_dim, n_heads = step
                yT = y.T.astype(BF16)
                sub = lax.broadcasted_iota(I32, (VT_PAD, tm), 0)
                ones_row = jnp.where(sub == 0, 1.0, 0.0).astype(BF16)
                for n in range(n_heads):
                    base = n * (head_dim + VT_PAD)
                    out[row, base:base + head_dim, :] = yT[n * head_dim:(n + 1) * head_dim]
                    out[row, base + head_dim:base + head_dim + VT_PAD, :] = ones_row
            elif kind == "silu":
                _, _, col = step
                out[:, col * LANES:(col + 1) * LANES] = y * (1.0 / (1.0 + jnp.exp(-y)))
            elif kind == "v2":
                _, _, row = step
                out[row] = jnp.where(lo_half, y, one_at_64).astype(BF16)
                out[row + 1] = jnp.where(lo_half, one_at_0, y).astype(BF16)
            elif kind == "iw":
                _, _, mult = step
                out[...] = (y * mult).T[0:8]
            else:
                raise ValueError(kind)


def _inproj(x2, norm_gain, w, gains, cos_t, sin_t, plan, out_defs, tm):
    t_tokens = x2.shape[0]
    s_len = cos_t.shape[0]
    n_cols = w.shape[1]
    n_tiles = t_tokens // tm
    s_tiles = s_len // tm
    bd = jnp.asarray(np.kron(np.eye(2), np.ones((HEAD_DIM, HEAD_DIM))), BF16)
    out_shapes, out_specs = [], []
    for kind, n, dtype in out_defs:
        if kind == "tok":
            out_shapes.append(jax.ShapeDtypeStruct((t_tokens, n), dtype))
            out_specs.append(pl.BlockSpec((tm, n), lambda i: (i, 0)))
        elif kind == "kt":
            out_shapes.append(jax.ShapeDtypeStruct((n, LANES, t_tokens), dtype))
            out_specs.append(pl.BlockSpec((n, LANES, tm), lambda i: (0, 0, i)))
        elif kind == "v":
            out_shapes.append(jax.ShapeDtypeStruct((n, t_tokens, LANES), dtype))
            out_specs.append(pl.BlockSpec((n, tm, LANES), lambda i: (0, i, 0)))
        elif kind == "vt":
            out_shapes.append(jax.ShapeDtypeStruct((n[0], n[1], t_tokens), dtype))
            out_specs.append(pl.BlockSpec((n[0], n[1], tm), lambda i: (0, 0, i)))
        elif kind == "rows":
            out_shapes.append(jax.ShapeDtypeStruct((n, t_tokens), dtype))
            out_specs.append(pl.BlockSpec((n, tm), lambda i: (0, i)))
    return pl.pallas_call(
        functools.partial(_inproj_kernel, plan),
        grid=(n_tiles,),
        in_specs=[
            pl.BlockSpec((tm, D_MODEL), lambda i: (i, 0)),
            pl.BlockSpec((1, D_MODEL), lambda i: (0, 0)),
            pl.BlockSpec((D_MODEL, n_cols), lambda i: (0, 0)),
            pl.BlockSpec((tm, LANES), lambda i: (i % s_tiles, 0)),
            pl.BlockSpec((tm, LANES), lambda i: (i % s_tiles, 0)),
            pl.BlockSpec(gains.shape, lambda i: (0, 0)),
            pl.BlockSpec((LANES, LANES), lambda i: (0, 0)),
        ],
        out_specs=out_specs,
        out_shape=out_shapes,
        compiler_params=_params("parallel"),
        name="inproj",
    )(x2, norm_gain.reshape(1, D_MODEL), w, cos_t, sin_t, gains, bd)


def _split_heads(qc):
    lane = lax.broadcasted_iota(I32, qc.shape, 1)
    zero = jnp.zeros_like(qc)
    return jnp.where(lane < HEAD_DIM, qc, zero), jnp.where(lane >= HEAD_DIM, qc, zero)


def _stack_heads(q_ref):
    tq, width = q_ref.shape
    parts = []
    for j in range(width // LANES):
        parts.extend(_split_heads(q_ref[:, j * LANES:(j + 1) * LANES]))
    return jnp.concatenate(parts, axis=0)


def _merge_heads(acc_lo, acc_hi):
    lane = lax.broadcasted_iota(I32, acc_lo.shape, 1)
    o_lo = acc_lo * (1.0 / acc_lo[:, HEAD_DIM:HEAD_DIM + 1])
    o_hi = acc_hi * (1.0 / acc_hi[:, 0:1])
    return jnp.where(lane < HEAD_DIM, o_lo, o_hi)


def _dsa_kernel(tq, tk, topk, s_len, qiT_ref, iwT_ref, ki_ref, qaT_ref, ka_ref, vaT_ref, o_ref,
                sc_ref, *flash_scratch):
    i = pl.program_id(1)
    q0 = i * tq
    nkb = (q0 + tq + tk - 1) // tk
    n_heads = 2 * N_PAIRS
    key_iota = lax.broadcasted_iota(I32, (tk, tq), 0)
    q_pos = q0 + lax.broadcasted_iota(I32, (tk, tq), 1)

    w = iwT_ref[...]
    qT_idx = jnp.concatenate([_split_heads_t(qiT_ref[j]) for j in range(N_PAIRS)], axis=1)

    def dots_into(dst, kb):
        dst[...] = jnp.dot(ki_ref[pl.ds(pl.multiple_of(kb * tk, tk), tk), :], qT_idx,
                           preferred_element_type=F32)

    def score_from(src, kb, last=False):
        k0 = pl.multiple_of(kb * tk, tk)
        score = jnp.zeros((tk, tq), F32)
        for h in range(n_heads):
            score = score + w[h:h + 1, :] * jnp.maximum(src[:, h * tq:(h + 1) * tq], 0.0)
        sc_ref[pl.ds(k0, tk), :] = jnp.where(key_iota + k0 <= q_pos, score, -jnp.inf)

    _skewed_blocks(0, nkb, dots_into, score_from, flash_scratch[0], flash_scratch[1])

    n_acc = 64

    def fold_keys(step, init, reduce):
        def blk(kb, a):
            k0 = pl.multiple_of(kb * tk, tk)
            return step(a, sc_ref[pl.ds(k0, tk), :].reshape(tk // n_acc, n_acc, tq), k0)
        a = lax.fori_loop(0, nkb, blk, jnp.full((n_acc, tq), init, F32))
        return reduce(a, axis=0, keepdims=True)

    def count(pred):
        return fold_keys(lambda a, x, k0: a + jnp.sum(jnp.where(pred(x, k0), 1.0, 0.0), axis=0),
                         0.0, jnp.sum)

    keep_all = (q_pos[0:1, :] + 1).astype(F32) <= topk
    row_max = fold_keys(lambda a, x, k0: jnp.maximum(a, jnp.max(x, axis=0)), -jnp.inf, jnp.max)
    row_min = fold_keys(
        lambda a, x, k0: jnp.minimum(a, jnp.min(jnp.where(x == -jnp.inf, jnp.inf, x), axis=0)),
        jnp.inf, jnp.min)

    RUN, DONE = 0.0, 1.0

    def probe(_, carry):
        lo, hi, state, thr = carry
        cand = lo + (hi - lo) * 0.5
        collapsed = (cand <= lo) | (cand >= hi)
        tot = count(lambda x, c0: x >= cand)
        running = state == RUN
        hit = running & (tot == topk)
        state = jnp.where(hit, DONE, jnp.where(running & collapsed, 2.0, state))
        thr = jnp.where(hit, cand, thr)
        return jnp.where(tot > topk, cand, lo), jnp.where(tot < topk, cand, hi), state, thr

    def n_running(state):
        return jnp.sum(jnp.where(state == RUN, 1.0, 0.0))

    def more_probes(carry):
        _, it, rest = carry
        rest = probe(0, probe(0, rest))
        return jnp.where(it + 2 >= MAX_PROBES, 0.0, n_running(rest[2])), it + 2, rest

    state0 = jnp.where(keep_all, DONE, RUN)
    hi0 = row_max + (jnp.abs(row_max) * 2.0 ** -20 + 1e-30)
    rest = lax.fori_loop(0, FIXED_PROBES, probe,
                         (row_min, hi0, state0, jnp.full((1, tq), -F32_MAX, F32)))
    _, _, (_, hi, state, thr) = lax.while_loop(
        lambda c: c[0] > 0.0, more_probes, (n_running(rest[2]), jnp.int32(FIXED_PROBES), rest))
    n_unresolved = jnp.sum(jnp.where(state == DONE, 0.0, 1.0))

    @pl.when(n_unresolved == 0.0)
    def _():
        def bias_block(kb, carry):
            k0 = pl.multiple_of(kb * tk, tk)
            sc_ref[pl.ds(k0, tk), :] = jnp.where(sc_ref[pl.ds(k0, tk), :] >= thr, 0.0, NEG)
            return carry
        lax.fori_loop(0, nkb, bias_block, 0)

    @pl.when(n_unresolved > 0.0)
    def _():
        def step_down(carry):
            _, hi, kth, found = carry
            below = fold_keys(
                lambda a, x, k0: jnp.maximum(a, jnp.max(jnp.where(x < hi, x, -jnp.inf), axis=0)),
                -jnp.inf, jnp.max)
            tot = count(lambda x, c0: x >= below)
            newly = (found == 0.0) & (tot >= topk)
            kth = jnp.where(newly, below, kth)
            found = jnp.where(newly, 1.0, found)
            return jnp.sum(1.0 - found), jnp.where(found == 0.0, below, hi), kth, found

        found0 = jnp.where(state == DONE, 1.0, 0.0)
        _, _, kth, _ = lax.while_loop(lambda c: c[0] > 0.0, step_down,
                                      (jnp.sum(1.0 - found0), hi, thr, found0))
        need = topk - count(lambda x, c0: x > kth)
        tri = jnp.where(lax.broadcasted_iota(I32, (tk, tk), 1) <= lax.broadcasted_iota(I32, (tk, tk), 0),
                        1.0, 0.0).astype(BF16)

        def bias_block(kb, ties_before):
            k0 = pl.multiple_of(kb * tk, tk)
            x = sc_ref[pl.ds(k0, tk), :]
            is_tie = x == kth
            rank = ties_before + jnp.dot(tri, jnp.where(is_tie, 1.0, 0.0).astype(BF16),
                                         preferred_element_type=F32)
            tie_bias = jnp.where(rank <= need, 0.0, NEG)
            sc_ref[pl.ds(k0, tk), :] = jnp.where(x > kth, 0.0, jnp.where(is_tie, tie_bias, NEG))
            return rank[tk - 1:tk, :]
        lax.fori_loop(0, nkb, bias_block, jnp.zeros((1, tq), F32))

    qT_s = jnp.concatenate([_split_heads_t(qaT_ref[j]) for j in range(N_PAIRS)], axis=1)

    def bias_of(kb, last):
        return sc_ref[pl.ds(pl.multiple_of(kb * tk, tk), tk), :]

    _skewed_flash_t(0, nkb, tk, qT_s, ka_ref, vaT_ref, bias_of, *flash_scratch,
                    bias_every_block=True)
    acc = flash_scratch[3][...]
    o = acc[0:HEAD_DIM] * (1.0 / acc[HEAD_DIM:HEAD_DIM + 1])
    for j in range(N_PAIRS):
        pair = jnp.concatenate([o[:, 2 * j * tq:(2 * j + 1) * tq],
                                o[:, (2 * j + 1) * tq:(2 * j + 2) * tq]], axis=0)
        o_ref[:, j * LANES:(j + 1) * LANES] = pair.T


def _dsa_attention(qiT, iwT, ki, qaT, ka, vaT, bsz, s_len, tq, tk):
    t_tokens = ki.shape[0]
    nq = s_len // tq
    topk = min(TOPK_MAX, s_len // 4)
    q_spec = pl.BlockSpec((N_PAIRS, LANES, tq), lambda b, i: (0, 0, b * nq + i))
    k_spec = pl.BlockSpec((s_len, LANES), lambda b, i: (b, 0))
    return pl.pallas_call(
        functools.partial(_dsa_kernel, tq, tk, float(topk), s_len),
        grid=(bsz, nq),
        in_specs=[
            q_spec,
            pl.BlockSpec((2 * N_PAIRS, tq), lambda b, i: (0, b * nq + i)),
            k_spec,
            q_spec,
            k_spec,
            pl.BlockSpec((1, vaT.shape[1], s_len), lambda b, i: (0, 0, b)),
        ],
        out_specs=pl.BlockSpec((tq, GROUP), lambda b, i: (b * nq + i, 0)),
        out_shape=jax.ShapeDtypeStruct((t_tokens, GROUP), F32),
        scratch_shapes=[pltpu.VMEM((s_len, tq), F32)]
        + _flash_t_scratch(tk, 2 * N_PAIRS * tq, vaT.shape[1]),
        compiler_params=_params("parallel", "parallel"),
        name="dsa_attention",
    )(qiT, iwT, ki, qaT, ka, vaT)


def _bias_tables(n, step, tq, tk, fn, keys_first=False):
    d = np.arange(n)[:, None, None] * step + np.arange(tq)[None, :, None] - np.arange(tk)[None, None, :]
    table = np.asarray(fn(d), np.float32)
    return jnp.asarray(np.ascontiguousarray(np.swapaxes(table, 1, 2)) if keys_first else table)


def _window_kernel(tq, q_ref, kT_ref, v_ref, bias_ref, sink_ref, o_ref):
    j = pl.program_id(1)
    i = pl.program_id(2)
    q0 = i * tq
    wk = tq + B_WINDOW
    k0 = pl.multiple_of(jnp.maximum(q0 - B_WINDOW, 0), LANES)
    kT = kT_ref[0, :, pl.ds(k0, wk)]
    bias = bias_ref[(q0 - k0) // B_WINDOW]
    lane = lax.broadcasted_iota(I32, (tq, LANES), 1)
    s_both = jnp.dot(_stack_heads(q_ref), kT, preferred_element_type=F32)
    accs = []
    for half in range(2):
        sink = sink_ref[2 * j + half] * LOG2E
        s = s_both[half * tq:(half + 1) * tq] + bias
        m = jnp.maximum(jnp.max(s, axis=1, keepdims=True), sink)
        p = jnp.exp2(s - m).astype(BF16)
        acc = jnp.dot(p, v_ref[half, pl.ds(k0, wk), :], preferred_element_type=F32)
        accs.append(acc + jnp.where(lane == (HEAD_DIM if half == 0 else 0), jnp.exp2(sink - m), 0.0))
    o_ref[...] = _merge_heads(*accs)


def _window_attention(q, kT, v, sinks, bsz, s_len, tq):
    t_tokens = q.shape[0]
    nq = s_len // tq
    bias = _bias_tables(2, B_WINDOW, tq, tq + B_WINDOW,
                        lambda d: np.where((d >= 0) & (d < B_WINDOW), 0.0, NEG))
    return pl.pallas_call(
        functools.partial(_window_kernel, tq),
        grid=(bsz, N_PAIRS, nq),
        in_specs=[
            pl.BlockSpec((tq, LANES), lambda b, j, i: (b * nq + i, j)),
            pl.BlockSpec((1, LANES, s_len), lambda b, j, i: (j // 2, 0, b)),
            pl.BlockSpec((2, s_len, LANES), lambda b, j, i: (j // 2, b, 0)),
            pl.BlockSpec(bias.shape, lambda b, j, i: (0, 0, 0)),
            pl.BlockSpec(memory_space=pltpu.SMEM),
        ],
        out_specs=pl.BlockSpec((tq, LANES), lambda b, j, i: (b * nq + i, j)),
        out_shape=jax.ShapeDtypeStruct((t_tokens, GROUP), F32),
        compiler_params=_params("parallel", "parallel", "parallel"),
        name="window_attention",
    )(q, kT, v, bias, sinks)


def _dilated_log2_multiplicity(d):
    count = np.zeros(d.shape)
    for window, dilation in C_PATTERNS:
        count += (d >= 0) & (d <= window) & (d % dilation == 0)
    return np.where(count > 0, np.log2(np.maximum(count, 1)), NEG)


def _split_heads_t(qT):
    row = lax.broadcasted_iota(I32, qT.shape, 0)
    zero = jnp.zeros_like(qT)
    return jnp.concatenate([jnp.where(row < HEAD_DIM, qT, zero), jnp.where(row >= HEAD_DIM, qT, zero)],
                           axis=1)


def _skewed_blocks(lo, hi, produce_into, consume_from, sa_ref, sb_ref):
    n = hi - lo
    produce_into(sa_ref, lo)

    def pair(pp, carry):
        kb = lo + 2 * pp
        produce_into(sb_ref, kb + 1)
        consume_from(sa_ref, kb)
        produce_into(sa_ref, kb + 2)
        consume_from(sb_ref, kb + 1)
        return carry

    lax.fori_loop(0, (n - 1) // 2, pair, 0)

    @pl.when(n % 2 == 1)
    def _():
        consume_from(sa_ref, hi - 1, last=True)

    @pl.when(n % 2 == 0)
    def _():
        produce_into(sb_ref, hi - 1)
        consume_from(sa_ref, hi - 2)
        consume_from(sb_ref, hi - 1, last=True)


def _skewed_flash_t(lo, hi, tk, qT_s, k_ref, vT_ref, bias_of, sa_ref, sb_ref, m_ref, acc_ref,
                    bias_every_block=False):
    m_ref[...] = jnp.full(m_ref.shape, NEG, F32)
    acc_ref[...] = jnp.zeros(acc_ref.shape, F32)

    def add_bias(s, bias):
        if bias is None:
            return s
        w = bias.shape[1]
        return jnp.concatenate([s[:, r * w:(r + 1) * w] + bias for r in range(s.shape[1] // w)],
                               axis=1)

    def logits_into(dst, kb):
        s = jnp.dot(k_ref[pl.ds(pl.multiple_of(kb * tk, tk), tk), :], qT_s,
                    preferred_element_type=F32)
        dst[...] = add_bias(s, bias_of(kb, False)) if bias_every_block else s

    def update_from(src, kb, last=False):
        s = src[...]
        if not bias_every_block:
            s = add_bias(s, bias_of(kb, last))
        m = m_ref[...]
        m_new = jnp.maximum(m, jnp.max(s, axis=0, keepdims=True))
        p = jnp.exp2(s - m_new).astype(BF16)
        vT = vT_ref[0, :, pl.ds(pl.multiple_of(kb * tk, tk), tk)]
        acc_ref[...] = jnp.exp2(m - m_new) * acc_ref[...] + jnp.dot(vT, p, preferred_element_type=F32)
        m_ref[...] = m_new

    _skewed_blocks(lo, hi, logits_into, update_from, sa_ref, sb_ref)


def _flash_t_scratch(tk, n_queries, n_value_rows):
    return [pltpu.VMEM((tk, n_queries), F32), pltpu.VMEM((tk, n_queries), F32),
            pltpu.VMEM((1, n_queries), F32), pltpu.VMEM((n_value_rows, n_queries), F32)]


def _dilated_kernel(t, reach, qT_ref, k_ref, vT_ref, bias_ref, o_ref, *scratch):
    i = pl.program_id(2)
    q0 = i * t
    qT_s = _split_heads_t(qT_ref[0])

    def bias_of(kb, last):
        return bias_ref[(q0 - kb * t) // t]

    _skewed_flash_t(jnp.maximum(q0 - reach, 0) // t, i + 1, t, qT_s, k_ref, vT_ref, bias_of, *scratch,
                    bias_every_block=True)
    acc = scratch[3][...]
    w = HEAD_DIM + VT_PAD
    o = jnp.concatenate(
        [acc[0:HEAD_DIM, 0:t] * (1.0 / acc[HEAD_DIM:HEAD_DIM + 1, 0:t]),
         acc[w:w + HEAD_DIM, t:2 * t] * (1.0 / acc[w + HEAD_DIM:w + HEAD_DIM + 1, t:2 * t])], axis=0)
    o_ref[...] = o.T


def _dilated_attention(qT, k, vT, bsz, s_len, t):
    t_tokens = k.shape[0]
    nq = s_len // t
    reach = max(wd for wd, _ in C_PATTERNS)
    bias = _bias_tables((reach + t - 1) // t + 1, t, t, t, _dilated_log2_multiplicity,
                        keys_first=True)
    return pl.pallas_call(
        functools.partial(_dilated_kernel, t, reach),
        grid=(bsz, N_PAIRS, nq),
        in_specs=[
            pl.BlockSpec((1, LANES, t), lambda b, j, i: (j, 0, b * nq + i)),
            pl.BlockSpec((s_len, LANES), lambda b, j, i: (b, j)),
            pl.BlockSpec((1, vT.shape[1], s_len), lambda b, j, i: (j, 0, b)),
            pl.BlockSpec(bias.shape, lambda b, j, i: (0, 0, 0)),
        ],
        out_specs=pl.BlockSpec((t, LANES), lambda b, j, i: (b * nq + i, j)),
        out_shape=jax.ShapeDtypeStruct((t_tokens, GROUP), F32),
        scratch_shapes=_flash_t_scratch(t, 2 * t, vT.shape[1]),
        compiler_params=_params("parallel", "parallel", "parallel"),
        name="dilated_attention",
    )(qT, k, vT, bias)


def _diff_kernel(t, lambda_init, qT_ref, k_ref, vT_ref, bias_ref, lq1_ref, lk1_ref, lq2_ref,
                 lk2_ref, sg_ref, o_ref, *scratch):
    i = pl.program_id(2)
    qT_s = _split_heads_t(qT_ref[0])

    def bias_of(kb, last):
        return bias_ref[...] if last else None

    _skewed_flash_t(0, i + 1, t, qT_s, k_ref, vT_ref, bias_of, *scratch)
    acc = scratch[3][...]
    o = acc[0:LANES] * (1.0 / acc[LANES:LANES + 1])
    lam = (jnp.exp(jnp.sum(lq1_ref[...] * lk1_ref[...], axis=1, keepdims=True))
           - jnp.exp(jnp.sum(lq2_ref[...] * lk2_ref[...], axis=1, keepdims=True)) + lambda_init)
    o = o[:, 0:t] - lam * o[:, t:2 * t]
    ms = jnp.mean(o * o, axis=0, keepdims=True)
    o_ref[...] = ((o * lax.rsqrt(ms + NORM_EPS) * sg_ref[...]) * (1.0 - lambda_init)).T


def _diff_attention(qT, k, vT, lq1, lk1, lq2, lk2, sub_gain, lambda_init, bsz, s_len, t):
    t_tokens = k.shape[0]
    nq = s_len // t
    bias = _bias_tables(1, t, t, t, lambda d: np.where(d >= 0, 0.0, NEG), keys_first=True)[0]
    vec = lambda a: a.reshape(1, -1).astype(F32)
    small = lambda n: pl.BlockSpec((1, n), lambda b, j, i: (0, 0))
    return pl.pallas_call(
        functools.partial(_diff_kernel, t, lambda_init),
        grid=(bsz, N_PAIRS, nq),
        in_specs=[
            pl.BlockSpec((1, LANES, t), lambda b, j, i: (j, 0, b * nq + i)),
            pl.BlockSpec((s_len, LANES), lambda b, j, i: (b, j)),
            pl.BlockSpec((1, vT.shape[1], s_len), lambda b, j, i: (j, 0, b)),
            pl.BlockSpec((t, t), lambda b, j, i: (0, 0)),
            small(HEAD_DIM), small(HEAD_DIM), small(HEAD_DIM), small(HEAD_DIM),
            pl.BlockSpec((LANES, 1), lambda b, j, i: (0, 0)),
        ],
        out_specs=pl.BlockSpec((t, LANES), lambda b, j, i: (b * nq + i, j)),
        out_shape=jax.ShapeDtypeStruct((t_tokens, GROUP), F32),
        scratch_shapes=_flash_t_scratch(t, 2 * t, vT.shape[1]),
        compiler_params=_params("parallel", "parallel", "parallel"),
        name="diff_attention",
    )(qT, k, vT, bias, vec(lq1), vec(lk1), vec(lq2), vec(lk2), sub_gain.reshape(-1, 1).astype(F32))


def _outproj_kernel(o1_ref, o2_ref, g_ref, x_ref, p_ref, wo_ref, pg_ref, wg_ref, wp_ref, out_ref):
    g = g_ref[...]
    y1 = (o1_ref[...] * g[:, 0:GROUP]).astype(BF16)
    y2 = (o2_ref[...] * g[:, GROUP:2 * GROUP]).astype(BF16)
    x1 = (x_ref[...]
          + jnp.dot(y1, wo_ref[0:GROUP, :], preferred_element_type=F32)
          + jnp.dot(y2, wo_ref[GROUP:2 * GROUP, :], preferred_element_type=F32))
    ms = jnp.mean(x1 * x1, axis=-1, keepdims=True)
    hn = (x1 * lax.rsqrt(ms + NORM_EPS) * pg_ref[...]).astype(BF16)
    z = jnp.dot(hn, wg_ref[...], preferred_element_type=F32)
    gate = 1.0 / (1.0 + jnp.exp(-z))
    pp = jnp.dot(p_ref[...].astype(BF16), wp_ref[...], preferred_element_type=F32)
    out_ref[...] = x1 + pp * gate


def _outproj(o1, o2, gate, x2, p2, w_out, ple_gain, w_gate, w_proj, tm):
    t_tokens = x2.shape[0]
    tok = lambda n: pl.BlockSpec((tm, n), lambda i: (i, 0))
    full = lambda a: pl.BlockSpec(a.shape, lambda i: (0, 0))
    pg = ple_gain.reshape(1, D_MODEL)
    return pl.pallas_call(
        _outproj_kernel,
        grid=(t_tokens // tm,),
        in_specs=[tok(GROUP), tok(GROUP), tok(2 * GROUP), tok(D_MODEL), tok(PLE_DIM),
                  full(w_out), full(pg), full(w_gate), full(w_proj)],
        out_specs=tok(D_MODEL),
        out_shape=jax.ShapeDtypeStruct((t_tokens, D_MODEL), F32),
        compiler_params=_params("parallel"),
        name="outproj_ple",
    )(o1, o2, gate, x2, p2, w_out, pg, w_gate, w_proj)


TM = 256
TQ_DSA = 128
TK_DSA = 512
TQ_WINDOW = 256
T_DILATED = 512
T_DIFF = 512


def _rope_tables(s_len):
    half = HEAD_DIM // 2
    inv = ROPE_THETA ** (-jnp.arange(half, dtype=F32) / half)
    ang = jnp.arange(s_len).astype(F32)[:, None] * inv[None, :]
    cos = jnp.tile(jnp.cos(ang), (1, 4))
    sin = jnp.sin(ang)
    sin_signed = jnp.tile(jnp.concatenate([-sin, sin], axis=1), (1, 2))
    return cos, sin_signed


def _pair_gain(g):
    return jnp.concatenate([g, g]).astype(F32)


def _even_layer(x2, p2, bsz, s_len, tables, norm_gain, w_in, w_out, a_q_gain, a_k_gain, idx_k_gain,
                b_q_gain, b_k_gain, b_sinks, ple_gain, w_gate, w_proj):
    hd = HEAD_DIM
    offs = np.cumsum([0, 512, hd, hd, 512, hd, 8, 512, 512, 2 * hd, 2 * hd, 512])
    aq, ak, av, iq, ik, iw, ag, bq, bk, bv, bg = [w_in[:, offs[n]:offs[n + 1]] for n in range(11)]
    dup = lambda c: jnp.concatenate([c, c], axis=1)
    w = jnp.concatenate(
        [aq, iq, bq, ag, bg,
         dup(ak), dup(ik), dup(bk[:, :hd]), dup(bk[:, hd:]),
         dup(av), dup(bv[:, :hd]), dup(bv[:, hd:]),
         jnp.pad(iw, ((0, 0), (0, LANES - 8)))], axis=1).astype(BF16)
    ones = jnp.ones((LANES,), F32)
    gains = jnp.stack(
        [_pair_gain(a_q_gain)] * 4 + [ones] * 4 + [_pair_gain(b_q_gain)] * 4 + [ones] * 8
        + [_pair_gain(a_k_gain), _pair_gain(idx_k_gain), _pair_gain(b_k_gain), _pair_gain(b_k_gain)]
        + [ones] * 4)
    idx_scale = (8 * hd) ** -0.5
    plan = ([("k", 0, r, True, QSCALE) for r in range(4)]
            + [("k", 1, r, False, 1.0) for r in range(4)]
            + [("q", 2, c, True, QSCALE) for c in range(4)]
            + [("silu", 3, c) for c in range(8)]
            + [("q", 4, 0, True, 1.0)]
            + [("q", 5, 0, True, 1.0)]
            + [("k", 6, r, True, 1.0) for r in range(2)]
            + [("vt", 7, 0, HEAD_DIM, 1)]
            + [("v2", 8, 2 * r) for r in range(2)]
            + [("iw", 9, idx_scale)])
    out_defs = [("kt", 4, BF16), ("kt", 4, BF16), ("tok", 512, BF16), ("tok", 1024, F32),
                ("tok", LANES, BF16), ("tok", LANES, BF16), ("kt", 2, BF16),
                ("vt", (1, HEAD_DIM + VT_PAD), BF16), ("v", 4, BF16), ("rows", 8, F32)]
    qaT, qiT, qb, gate, ka, ki, kbT, vaT, vb, iwT = _inproj(x2, norm_gain, w, gains, *tables, plan,
                                                            out_defs, TM)
    oa = _dsa_attention(qiT, iwT, ki, qaT, ka, vaT, bsz, s_len, min(TQ_DSA, s_len), TK_DSA)
    ob = _window_attention(qb, kbT, vb, b_sinks.astype(F32), bsz, s_len, TQ_WINDOW)
    return _outproj(oa, ob, gate, x2, p2, w_out.astype(BF16), ple_gain, w_gate.astype(BF16),
                    w_proj.astype(BF16), TM)


def _odd_layer(x2, p2, bsz, s_len, tables, norm_gain, w_in, w_out, c_q_gain, c_k_gain, d_q_gain,
               d_k_gain, lq1, lk1, lq2, lk2, sub_gain, lambda_init, ple_gain, w_gate, w_proj):
    ones = jnp.ones((LANES,), F32)
    gains = jnp.stack([_pair_gain(c_q_gain)] * 4 + [_pair_gain(c_k_gain)] * 4 + [ones] * 8
                      + [_pair_gain(d_q_gain)] * 4 + [_pair_gain(d_k_gain)] * 4 + [ones] * 8)
    plan = ([("k", 0, r, True, QSCALE) for r in range(4)]
            + [("q", 1, c, True, 1.0) for c in range(4)]
            + [("vt", 2, r, HEAD_DIM, 2) for r in range(4)]
            + [("silu", 3, c) for c in range(4)]
            + [("k", 4, r, True, QSCALE) for r in range(4)]
            + [("q", 5, c, True, 1.0) for c in range(4)]
            + [("vt", 6, r, 2 * HEAD_DIM, 1) for r in range(4)]
            + [("silu", 3, 4 + c) for c in range(4)])
    out_defs = [("kt", 4, BF16), ("tok", 512, BF16), ("vt", (4, 2 * (HEAD_DIM + VT_PAD)), BF16),
                ("tok", 1024, F32),
                ("kt", 4, BF16), ("tok", 512, BF16), ("vt", (4, 2 * HEAD_DIM + VT_PAD), BF16)]
    qcT, kc, vcT, gate, qdT, kd, vdT = _inproj(x2, norm_gain, w_in.astype(BF16), gains, *tables,
                                               plan, out_defs, TM)
    oc = _dilated_attention(qcT, kc, vcT, bsz, s_len, T_DILATED)
    od = _diff_attention(qdT, kd, vdT, lq1, lk1, lq2, lk2, sub_gain, lambda_init, bsz, s_len, T_DIFF)
    return _outproj(oc, od, gate, x2, p2, w_out.astype(BF16), ple_gain, w_gate.astype(BF16),
                    w_proj.astype(BF16), TM)


def kernel(x, p, norm_gain, w_in_even, w_out_even, a_q_gain, a_k_gain, idx_k_gain, b_q_gain, b_k_gain, b_sinks, w_in_odd, w_out_odd, c_q_gain, c_k_gain, d_q_gain, d_k_gain, d_lambda_q1, d_lambda_k1, d_lambda_q2, d_lambda_k2, d_subln_gain, ple_norm_gain, w_ple_gate, w_ple_proj):
    bsz, s_len, d_model = x.shape
    depth = p.shape[0]
    assert d_model == D_MODEL and s_len % 512 == 0
    tables = _rope_tables(s_len)
    x2 = x.reshape(bsz * s_len, d_model)
    for i in range(depth):
        j = i // 2
        p2 = p[i].reshape(bsz * s_len, PLE_DIM)
        if i % 2 == 0:
            x2 = _even_layer(x2, p2, bsz, s_len, tables, norm_gain[i], w_in_even[j], w_out_even[j],
                             a_q_gain[j], a_k_gain[j], idx_k_gain[j], b_q_gain[j], b_k_gain[j],
                             b_sinks[j], ple_norm_gain[i], w_ple_gate[i], w_ple_proj[i])
        else:
            lambda_init = 0.8 - 0.6 * math.exp(-0.3 * i)
            x2 = _odd_layer(x2, p2, bsz, s_len, tables, norm_gain[i], w_in_odd[j], w_out_odd[j],
                            c_q_gain[j], c_k_gain[j], d_q_gain[j], d_k_gain[j], d_lambda_q1[j],
                            d_lambda_k1[j], d_lambda_q2[j], d_lambda_k2[j], d_subln_gain[j],
                            lambda_init, ple_norm_gain[i], w_ple_gate[i], w_ple_proj[i])
    return x2.reshape(bsz, s_len, d_model)
```

```python
import functools
import math

import numpy as np
import jax
import jax.numpy as jnp
from jax import lax
from jax.experimental import pallas as pl
from jax.experimental.pallas import tpu as pltpu

F32 = jnp.float32
BF16 = jnp.bfloat16
I32 = jnp.int32
ACT = jnp.bfloat16

D_MODEL = 1024
HEAD_DIM = 64
ROPE_THETA = 10000.0
NORM_EPS = 1e-6
PLE_DIM = 256
TOPK_MAX = 256
B_WINDOW = 128
C_PATTERNS = ((128, 1), (512, 4), (2048, 16))
N_PAIRS = 4
LANES = 128
GROUP = 512
VT_PAD = 16
LOG2E = 1.4426950408889634
QSCALE = HEAD_DIM ** -0.5 * LOG2E
NEG = -1e30
F32_MAX = float(np.finfo(np.float32).max)
FIXED_PROBES = 16
MAX_PROBES = 20
VMEM_LIMIT = 48 * 1024 * 1024


def _params(*sem):
    return pltpu.CompilerParams(dimension_semantics=sem, vmem_limit_bytes=VMEM_LIMIT)


def _norm_rope(y, gain, cos, sin_signed, bd, first_half, norm):
    if norm:
        ss = y * y
        hi = ss.astype(BF16)
        lo = (ss - hi.astype(F32)).astype(BF16)
        seg = (jnp.dot(hi, bd, preferred_element_type=F32)
               + jnp.dot(lo, bd, preferred_element_type=F32))
        y = y * lax.rsqrt(seg * (1.0 / HEAD_DIM) + NORM_EPS) * gain
    rot = jnp.where(first_half, pltpu.roll(y, 96, 1), pltpu.roll(y, 32, 1))
    return y * cos + rot * sin_signed


def _inproj_kernel(plan, x_ref, ng_ref, w_ref, cos_ref, sin_ref, gains_ref, bd_ref, *outs):
    x = x_ref[...]
    ms = jnp.mean(x * x, axis=-1, keepdims=True)
    h = (x * lax.rsqrt(ms + NORM_EPS) * ng_ref[...]).astype(BF16)
    cos = cos_ref[...]
    sin = sin_ref[...]
    bd = bd_ref[...]
    tm = x.shape[0]
    lane = lax.broadcasted_iota(I32, (tm, LANES), 1)
    lo_half = lane < HEAD_DIM
    first_half = (lane & (HEAD_DIM - 1)) < HEAD_DIM // 2
    one_at_64 = jnp.where(lane == HEAD_DIM, 1.0, 0.0)
    one_at_0 = jnp.where(lane == 0, 1.0, 0.0)
    for g in range(len(plan) // 4):
        yg = jnp.dot(h, w_ref[:, g * GROUP:(g + 1) * GROUP], preferred_element_type=F32)
        for c4 in range(4):
            c = g * 4 + c4
            y = yg[:, c4 * LANES:(c4 + 1) * LANES]
            step = plan[c]
            kind, out = step[0], outs[step[1]]
            if kind == "q":
                _, _, col, norm, mult = step
                y = _norm_rope(y, gains_ref[c:c + 1, :], cos, sin, bd, first_half, norm)
                out[:, col * LANES:(col + 1) * LANES] = (y * mult).astype(BF16)
            elif kind == "k":
                _, _, row, norm, mult = step
                y = _norm_rope(y, gains_ref[c:c + 1, :], cos, sin, bd, first_half, norm)
                out[row] = (y * mult).T.astype(BF16)
            elif kind == "vt":
                _, _, row, head_dim, n_heads = step
                yT = y.T.astype(BF16)
                sub = lax.broadcasted_iota(I32, (VT_PAD, tm), 0)
                ones_row = jnp.where(sub == 0, 1.0, 0.0).astype(BF16)
                for n in range(n_heads):
                    base = n * (head_dim + VT_PAD)
                    out[row, base:base + head_dim, :] = yT[n * head_dim:(n + 1) * head_dim]
                    out[row, base + head_dim:base + head_dim + VT_PAD, :] = ones_row
            elif kind == "silu":
                _, _, col = step
                out[:, col * LANES:(col + 1) * LANES] = (y * (1.0 / (1.0 + jnp.exp(-y)))).astype(out.dtype)
            elif kind == "v2":
                _, _, row = step
                out[row] = jnp.where(lo_half, y, one_at_64).astype(BF16)
                out[row + 1] = jnp.where(lo_half, one_at_0, y).astype(BF16)
            elif kind == "iw":
                _, _, mult = step
                out[...] = (y * mult).T[0:8]
            else:
                raise ValueError(kind)


def _inproj(x2, norm_gain, w, gains, cos_t, sin_t, plan, out_defs, tm):
    t_tokens = x2.shape[0]
    s_len = cos_t.shape[0]
    n_cols = w.shape[1]
    n_tiles = t_tokens // tm
    s_tiles = s_len // tm
    bd = jnp.asarray(np.kron(np.eye(2), np.ones((HEAD_DIM, HEAD_DIM))), BF16)
    out_shapes, out_specs = [], []
    for kind, n, dtype in out_defs:
        if kind == "tok":
            out_shapes.append(jax.ShapeDtypeStruct((t_tokens, n), dtype))
            out_specs.append(pl.BlockSpec((tm, n), lambda i: (i, 0)))
        elif kind == "kt":
            out_shapes.append(jax.ShapeDtypeStruct((n, LANES, t_tokens), dtype))
            out_specs.append(pl.BlockSpec((n, LANES, tm), lambda i: (0, 0, i)))
        elif kind == "v":
            out_shapes.append(jax.ShapeDtypeStruct((n, t_tokens, LANES), dtype))
            out_specs.append(pl.BlockSpec((n, tm, LANES), lambda i: (0, i, 0)))
        elif kind == "vt":
            out_shapes.append(jax.ShapeDtypeStruct((n[0], n[1], t_tokens), dtype))
            out_specs.append(pl.BlockSpec((n[0], n[1], tm), lambda i: (0, 0, i)))
        elif kind == "rows":
            out_shapes.append(jax.ShapeDtypeStruct((n, t_tokens), dtype))
            out_specs.append(pl.BlockSpec((n, tm), lambda i: (0, i)))
    return pl.pallas_call(
        functools.partial(_inproj_kernel, plan),
        grid=(n_tiles,),
        in_specs=[
            pl.BlockSpec((tm, D_MODEL), lambda i: (i, 0)),
            pl.BlockSpec((1, D_MODEL), lambda i: (0, 0)),
            pl.BlockSpec((D_MODEL, n_cols), lambda i: (0, 0)),
            pl.BlockSpec((tm, LANES), lambda i: (i % s_tiles, 0)),
            pl.BlockSpec((tm, LANES), lambda i: (i % s_tiles, 0)),
            pl.BlockSpec(gains.shape, lambda i: (0, 0)),
            pl.BlockSpec((LANES, LANES), lambda i: (0, 0)),
        ],
        out_specs=out_specs,
        out_shape=out_shapes,
        compiler_params=_params("parallel"),
        name="inproj",
    )(x2, norm_gain.reshape(1, D_MODEL), w, cos_t, sin_t, gains, bd)


def _split_heads(qc):
    lane = lax.broadcasted_iota(I32, qc.shape, 1)
    zero = jnp.zeros_like(qc)
    return jnp.where(lane < HEAD_DIM, qc, zero), jnp.where(lane >= HEAD_DIM, qc, zero)


def _stack_heads(q_ref):
    tq, width = q_ref.shape
    parts = []
    for j in range(width // LANES):
        parts.extend(_split_heads(q_ref[:, j * LANES:(j + 1) * LANES]))
    return jnp.concatenate(parts, axis=0)


def _merge_heads(acc_lo, acc_hi):
    lane = lax.broadcasted_iota(I32, acc_lo.shape, 1)
    o_lo = acc_lo * (1.0 / acc_lo[:, HEAD_DIM:HEAD_DIM + 1])
    o_hi = acc_hi * (1.0 / acc_hi[:, 0:1])
    return jnp.where(lane < HEAD_DIM, o_lo, o_hi)


def _dsa_kernel(tq, tk, topk, s_len, qiT_ref, iwT_ref, ki_ref, qaT_ref, ka_ref, vaT_ref, o_ref,
                sc_ref, *flash_scratch):
    i = pl.program_id(1)
    q0 = i * tq
    nkb = (q0 + tq + tk - 1) // tk
    n_heads = 2 * N_PAIRS
    key_iota = lax.broadcasted_iota(I32, (tk, tq), 0)
    q_pos = q0 + lax.broadcasted_iota(I32, (tk, tq), 1)

    w = iwT_ref[...]
    qT_idx = jnp.concatenate([_split_heads_t(qiT_ref[j]) for j in range(N_PAIRS)], axis=1)

    def dots_into(dst, kb):
        dst[...] = jnp.dot(ki_ref[pl.ds(pl.multiple_of(kb * tk, tk), tk), :], qT_idx,
                           preferred_element_type=F32)

    def score_from(src, kb, last=False):
        k0 = pl.multiple_of(kb * tk, tk)
        score = jnp.zeros((tk, tq), F32)
        for h in range(n_heads):
            score = score + w[h:h + 1, :] * jnp.maximum(src[:, h * tq:(h + 1) * tq], 0.0)
        sc_ref[pl.ds(k0, tk), :] = jnp.where(key_iota + k0 <= q_pos, score, -jnp.inf)

    _skewed_blocks(0, nkb, dots_into, score_from, flash_scratch[0], flash_scratch[1])

    n_acc = 64

    def fold_keys(step, init, reduce):
        def blk(kb, a):
            k0 = pl.multiple_of(kb * tk, tk)
            return step(a, sc_ref[pl.ds(k0, tk), :].reshape(tk // n_acc, n_acc, tq), k0)
        a = lax.fori_loop(0, nkb, blk, jnp.full((n_acc, tq), init, F32))
        return reduce(a, axis=0, keepdims=True)

    def count(pred):
        return fold_keys(lambda a, x, k0: a + jnp.sum(jnp.where(pred(x, k0), 1.0, 0.0), axis=0),
                         0.0, jnp.sum)

    keep_all = (q_pos[0:1, :] + 1).astype(F32) <= topk
    row_max = fold_keys(lambda a, x, k0: jnp.maximum(a, jnp.max(x, axis=0)), -jnp.inf, jnp.max)
    row_min = fold_keys(
        lambda a, x, k0: jnp.minimum(a, jnp.min(jnp.where(x == -jnp.inf, jnp.inf, x), axis=0)),
        jnp.inf, jnp.min)

    RUN, DONE = 0.0, 1.0

    def probe(_, carry):
        lo, hi, state, thr = carry
        cand = lo + (hi - lo) * 0.5
        collapsed = (cand <= lo) | (cand >= hi)
        tot = count(lambda x, c0: x >= cand)
        running = state == RUN
        hit = running & (tot == topk)
        state = jnp.where(hit, DONE, jnp.where(running & collapsed, 2.0, state))
        thr = jnp.where(hit, cand, thr)
        return jnp.where(tot > topk, cand, lo), jnp.where(tot < topk, cand, hi), state, thr

    def n_running(state):
        return jnp.sum(jnp.where(state == RUN, 1.0, 0.0))

    def more_probes(carry):
        _, it, rest = carry
        rest = probe(0, probe(0, rest))
        return jnp.where(it + 2 >= MAX_PROBES, 0.0, n_running(rest[2])), it + 2, rest

    state0 = jnp.where(keep_all, DONE, RUN)
    hi0 = row_max + (jnp.abs(row_max) * 2.0 ** -20 + 1e-30)
    rest = lax.fori_loop(0, FIXED_PROBES, probe,
                         (row_min, hi0, state0, jnp.full((1, tq), -F32_MAX, F32)))
    _, _, (_, hi, state, thr) = lax.while_loop(
        lambda c: c[0] > 0.0, more_probes, (n_running(rest[2]), jnp.int32(FIXED_PROBES), rest))
    n_unresolved = jnp.sum(jnp.where(state == DONE, 0.0, 1.0))

    @pl.when(n_unresolved == 0.0)
    def _():
        def bias_block(kb, carry):
            k0 = pl.multiple_of(kb * tk, tk)
            sc_ref[pl.ds(k0, tk), :] = jnp.where(sc_ref[pl.ds(k0, tk), :] >= thr, 0.0, NEG)
            return carry
        lax.fori_loop(0, nkb, bias_block, 0)

    @pl.when(n_unresolved > 0.0)
    def _():
        def step_down(carry):
            _, hi, kth, found = carry
            below = fold_keys(
                lambda a, x, k0: jnp.maximum(a, jnp.max(jnp.where(x < hi, x, -jnp.inf), axis=0)),
                -jnp.inf, jnp.max)
            tot = count(lambda x, c0: x >= below)
            newly = (found == 0.0) & (tot >= topk)
            kth = jnp.where(newly, below, kth)
            found = jnp.where(newly, 1.0, found)
            return jnp.sum(1.0 - found), jnp.where(found == 0.0, below, hi), kth, found

        found0 = jnp.where(state == DONE, 1.0, 0.0)
        _, _, kth, _ = lax.while_loop(lambda c: c[0] > 0.0, step_down,
                                      (jnp.sum(1.0 - found0), hi, thr, found0))
        need = topk - count(lambda x, c0: x > kth)
        tri = jnp.where(lax.broadcasted_iota(I32, (tk, tk), 1) <= lax.broadcasted_iota(I32, (tk, tk), 0),
                        1.0, 0.0).astype(BF16)

        def bias_block(kb, ties_before):
            k0 = pl.multiple_of(kb * tk, tk)
            x = sc_ref[pl.ds(k0, tk), :]
            is_tie = x == kth
            rank = ties_before + jnp.dot(tri, jnp.where(is_tie, 1.0, 0.0).astype(BF16),
                                         preferred_element_type=F32)
            tie_bias = jnp.where(rank <= need, 0.0, NEG)
            sc_ref[pl.ds(k0, tk), :] = jnp.where(x > kth, 0.0, jnp.where(is_tie, tie_bias, NEG))
            return rank[tk - 1:tk, :]
        lax.fori_loop(0, nkb, bias_block, jnp.zeros((1, tq), F32))

    qT_s = jnp.concatenate([_split_heads_t(qaT_ref[j]) for j in range(N_PAIRS)], axis=1)

    def bias_of(kb, last):
        return sc_ref[pl.ds(pl.multiple_of(kb * tk, tk), tk), :]

    _skewed_flash_t(0, nkb, tk, qT_s, ka_ref, vaT_ref, bias_of, *flash_scratch,
                    bias_every_block=True)
    acc = flash_scratch[3][...]
    o = acc[0:HEAD_DIM] * (1.0 / acc[HEAD_DIM:HEAD_DIM + 1])
    for j in range(N_PAIRS):
        pair = jnp.concatenate([o[:, 2 * j * tq:(2 * j + 1) * tq],
                                o[:, (2 * j + 1) * tq:(2 * j + 2) * tq]], axis=0)
        o_ref[:, j * LANES:(j + 1) * LANES] = pair.T.astype(o_ref.dtype)


def _dsa_attention(qiT, iwT, ki, qaT, ka, vaT, bsz, s_len, tq, tk):
    t_tokens = ki.shape[0]
    nq = s_len // tq
    topk = min(TOPK_MAX, s_len // 4)
    q_spec = pl.BlockSpec((N_PAIRS, LANES, tq), lambda b, i: (0, 0, b * nq + i))
    k_spec = pl.BlockSpec((s_len, LANES), lambda b, i: (b, 0))
    return pl.pallas_call(
        functools.partial(_dsa_kernel, tq, tk, float(topk), s_len),
        grid=(bsz, nq),
        in_specs=[
            q_spec,
            pl.BlockSpec((2 * N_PAIRS, tq), lambda b, i: (0, b * nq + i)),
            k_spec,
            q_spec,
            k_spec,
            pl.BlockSpec((1, vaT.shape[1], s_len), lambda b, i: (0, 0, b)),
        ],
        out_specs=pl.BlockSpec((tq, GROUP), lambda b, i: (b * nq + i, 0)),
        out_shape=jax.ShapeDtypeStruct((t_tokens, GROUP), ACT),
        scratch_shapes=[pltpu.VMEM((s_len, tq), F32)]
        + _flash_t_scratch(tk, 2 * N_PAIRS * tq, vaT.shape[1]),
        compiler_params=_params("parallel", "parallel"),
        name="dsa_attention",
    )(qiT, iwT, ki, qaT, ka, vaT)


def _bias_tables(n, step, tq, tk, fn, keys_first=False):
    d = np.arange(n)[:, None, None] * step + np.arange(tq)[None, :, None] - np.arange(tk)[None, None, :]
    table = np.asarray(fn(d), np.float32)
    return jnp.asarray(np.ascontiguousarray(np.swapaxes(table, 1, 2)) if keys_first else table)


def _window_kernel(tq, q_ref, kT_ref, v_ref, bias_ref, sink_ref, o_ref):
    j = pl.program_id(1)
    i = pl.program_id(2)
    q0 = i * tq
    wk = tq + B_WINDOW
    k0 = pl.multiple_of(jnp.maximum(q0 - B_WINDOW, 0), LANES)
    kT = kT_ref[0, :, pl.ds(k0, wk)]
    bias = bias_ref[(q0 - k0) // B_WINDOW]
    lane = lax.broadcasted_iota(I32, (tq, LANES), 1)
    s_both = jnp.dot(_stack_heads(q_ref), kT, preferred_element_type=F32)
    accs = []
    for half in range(2):
        sink = sink_ref[2 * j + half] * LOG2E
        s = s_both[half * tq:(half + 1) * tq] + bias
        m = jnp.maximum(jnp.max(s, axis=1, keepdims=True), sink)
        p = jnp.exp2(s - m).astype(BF16)
        acc = jnp.dot(p, v_ref[half, pl.ds(k0, wk), :], preferred_element_type=F32)
        accs.append(acc + jnp.where(lane == (HEAD_DIM if half == 0 else 0), jnp.exp2(sink - m), 0.0))
    o_ref[...] = _merge_heads(*accs).astype(o_ref.dtype)


def _window_attention(q, kT, v, sinks, bsz, s_len, tq):
    t_tokens = q.shape[0]
    nq = s_len // tq
    bias = _bias_tables(2, B_WINDOW, tq, tq + B_WINDOW,
                        lambda d: np.where((d >= 0) & (d < B_WINDOW), 0.0, NEG))
    return pl.pallas_call(
        functools.partial(_window_kernel, tq),
        grid=(bsz, N_PAIRS, nq),
        in_specs=[
            pl.BlockSpec((tq, LANES), lambda b, j, i: (b * nq + i, j)),
            pl.BlockSpec((1, LANES, s_len), lambda b, j, i: (j // 2, 0, b)),
            pl.BlockSpec((2, s_len, LANES), lambda b, j, i: (j // 2, b, 0)),
            pl.BlockSpec(bias.shape, lambda b, j, i: (0, 0, 0)),
            pl.BlockSpec(memory_space=pltpu.SMEM),
        ],
        out_specs=pl.BlockSpec((tq, LANES), lambda b, j, i: (b * nq + i, j)),
        out_shape=jax.ShapeDtypeStruct((t_tokens, GROUP), ACT),
        compiler_params=_params("parallel", "parallel", "parallel"),
        name="window_attention",
    )(q, kT, v, bias, sinks)


def _dilated_log2_multiplicity(d):
    count = np.zeros(d.shape)
    for window, dilation in C_PATTERNS:
        count += (d >= 0) & (d <= window) & (d % dilation == 0)
    return np.where(count > 0, np.log2(np.maximum(count, 1)), NEG)


def _split_heads_t(qT):
    row = lax.broadcasted_iota(I32, qT.shape, 0)
    zero = jnp.zeros_like(qT)
    return jnp.concatenate([jnp.where(row < HEAD_DIM, qT, zero), jnp.where(row >= HEAD_DIM, qT, zero)],
                           axis=1)


def _skewed_blocks(lo, hi, produce_into, consume_from, sa_ref, sb_ref):
    n = hi - lo
    produce_into(sa_ref, lo)

    def pair(pp, carry):
        kb = lo + 2 * pp
        produce_into(sb_ref, kb + 1)
        consume_from(sa_ref, kb)
        produce_into(sa_ref, kb + 2)
        consume_from(sb_ref, kb + 1)
        return carry

    lax.fori_loop(0, (n - 1) // 2, pair, 0)

    @pl.when(n % 2 == 1)
    def _():
        consume_from(sa_ref, hi - 1, last=True)

    @pl.when(n % 2 == 0)
    def _():
        produce_into(sb_ref, hi - 1)
        consume_from(sa_ref, hi - 2)
        consume_from(sb_ref, hi - 1, last=True)


def _skewed_flash_t(lo, hi, tk, qT_s, k_ref, vT_ref, bias_of, sa_ref, sb_ref, m_ref, acc_ref,
                    bias_every_block=False):
    m_ref[...] = jnp.full(m_ref.shape, NEG, F32)
    acc_ref[...] = jnp.zeros(acc_ref.shape, F32)

    def add_bias(s, bias):
        if bias is None:
            return s
        w = bias.shape[1]
        return jnp.concatenate([s[:, r * w:(r + 1) * w] + bias for r in range(s.shape[1] // w)],
                               axis=1)

    def logits_into(dst, kb):
        s = jnp.dot(k_ref[pl.ds(pl.multiple_of(kb * tk, tk), tk), :], qT_s,
                    preferred_element_type=F32)
        dst[...] = add_bias(s, bias_of(kb, False)) if bias_every_block else s

    def update_from(src, kb, last=False):
        s = src[...]
        if not bias_every_block:
            s = add_bias(s, bias_of(kb, last))
        m = m_ref[...]
        m_new = jnp.maximum(m, jnp.max(s, axis=0, keepdims=True))
        p = jnp.exp2(s - m_new).astype(BF16)
        vT = vT_ref[0, :, pl.ds(pl.multiple_of(kb * tk, tk), tk)]
        acc_ref[...] = jnp.exp2(m - m_new) * acc_ref[...] + jnp.dot(vT, p, preferred_element_type=F32)
        m_ref[...] = m_new

    _skewed_blocks(lo, hi, logits_into, update_from, sa_ref, sb_ref)


def _flash_t_scratch(tk, n_queries, n_value_rows):
    return [pltpu.VMEM((tk, n_queries), F32), pltpu.VMEM((tk, n_queries), F32),
            pltpu.VMEM((1, n_queries), F32), pltpu.VMEM((n_value_rows, n_queries), F32)]


def _dilated_kernel(t, reach, qT_ref, k_ref, vT_ref, bias_ref, o_ref, *scratch):
    i = pl.program_id(2)
    q0 = i * t
    qT_s = _split_heads_t(qT_ref[0])

    def bias_of(kb, last):
        return bias_ref[(q0 - kb * t) // t]

    _skewed_flash_t(jnp.maximum(q0 - reach, 0) // t, i + 1, t, qT_s, k_ref, vT_ref, bias_of, *scratch,
                    bias_every_block=True)
    acc = scratch[3][...]
    w = HEAD_DIM + VT_PAD
    o = jnp.concatenate(
        [acc[0:HEAD_DIM, 0:t] * (1.0 / acc[HEAD_DIM:HEAD_DIM + 1, 0:t]),
         acc[w:w + HEAD_DIM, t:2 * t] * (1.0 / acc[w + HEAD_DIM:w + HEAD_DIM + 1, t:2 * t])], axis=0)
    o_ref[...] = o.T.astype(o_ref.dtype)


def _dilated_attention(qT, k, vT, bsz, s_len, t):
    t_tokens = k.shape[0]
    nq = s_len // t
    reach = max(wd for wd, _ in C_PATTERNS)
    bias = _bias_tables((reach + t - 1) // t + 1, t, t, t, _dilated_log2_multiplicity,
                        keys_first=True)
    return pl.pallas_call(
        functools.partial(_dilated_kernel, t, reach),
        grid=(bsz, N_PAIRS, nq),
        in_specs=[
            pl.BlockSpec((1, LANES, t), lambda b, j, i: (j, 0, b * nq + i)),
            pl.BlockSpec((s_len, LANES), lambda b, j, i: (b, j)),
            pl.BlockSpec((1, vT.shape[1], s_len), lambda b, j, i: (j, 0, b)),
            pl.BlockSpec(bias.shape, lambda b, j, i: (0, 0, 0)),
        ],
        out_specs=pl.BlockSpec((t, LANES), lambda b, j, i: (b * nq + i, j)),
        out_shape=jax.ShapeDtypeStruct((t_tokens, GROUP), ACT),
        scratch_shapes=_flash_t_scratch(t, 2 * t, vT.shape[1]),
        compiler_params=_params("parallel", "parallel", "parallel"),
        name="dilated_attention",
    )(qT, k, vT, bias)


def _diff_kernel(t, lambda_init, qT_ref, k_ref, vT_ref, bias_ref, lq1_ref, lk1_ref, lq2_ref,
                 lk2_ref, sg_ref, o_ref, *scratch):
    i = pl.program_id(2)
    qT_s = _split_heads_t(qT_ref[0])

    def bias_of(kb, last):
        return bias_ref[...] if last else None

    _skewed_flash_t(0, i + 1, t, qT_s, k_ref, vT_ref, bias_of, *scratch)
    acc = scratch[3][...]
    o = acc[0:LANES] * (1.0 / acc[LANES:LANES + 1])
    lam = (jnp.exp(jnp.sum(lq1_ref[...] * lk1_ref[...], axis=1, keepdims=True))
           - jnp.exp(jnp.sum(lq2_ref[...] * lk2_ref[...], axis=1, keepdims=True)) + lambda_init)
    o = o[:, 0:t] - lam * o[:, t:2 * t]
    ms = jnp.mean(o * o, axis=0, keepdims=True)
    o_ref[...] = ((o * lax.rsqrt(ms + NORM_EPS) * sg_ref[...]) * (1.0 - lambda_init)).T.astype(
        o_ref.dtype)


def _diff_attention(qT, k, vT, lq1, lk1, lq2, lk2, sub_gain, lambda_init, bsz, s_len, t):
    t_tokens = k.shape[0]
    nq = s_len // t
    bias = _bias_tables(1, t, t, t, lambda d: np.where(d >= 0, 0.0, NEG), keys_first=True)[0]
    vec = lambda a: a.reshape(1, -1).astype(F32)
    small = lambda n: pl.BlockSpec((1, n), lambda b, j, i: (0, 0))
    return pl.pallas_call(
        functools.partial(_diff_kernel, t, lambda_init),
        grid=(bsz, N_PAIRS, nq),
        in_specs=[
            pl.BlockSpec((1, LANES, t), lambda b, j, i: (j, 0, b * nq + i)),
            pl.BlockSpec((s_len, LANES), lambda b, j, i: (b, j)),
            pl.BlockSpec((1, vT.shape[1], s_len), lambda b, j, i: (j, 0, b)),
            pl.BlockSpec((t, t), lambda b, j, i: (0, 0)),
            small(HEAD_DIM), small(HEAD_DIM), small(HEAD_DIM), small(HEAD_DIM),
            pl.BlockSpec((LANES, 1), lambda b, j, i: (0, 0)),
        ],
        out_specs=pl.BlockSpec((t, LANES), lambda b, j, i: (b * nq + i, j)),
        out_shape=jax.ShapeDtypeStruct((t_tokens, GROUP), ACT),
        scratch_shapes=_flash_t_scratch(t, 2 * t, vT.shape[1]),
        compiler_params=_params("parallel", "parallel", "parallel"),
        name="diff_attention",
    )(qT, k, vT, bias, vec(lq1), vec(lk1), vec(lq2), vec(lk2), sub_gain.reshape(-1, 1).astype(F32))


def _outproj_kernel(o1_ref, o2_ref, g_ref, x_ref, p_ref, wo_ref, pg_ref, wg_ref, wp_ref, out_ref):
    g = g_ref[...]
    y1 = (o1_ref[...] * g[:, 0:GROUP]).astype(BF16)
    y2 = (o2_ref[...] * g[:, GROUP:2 * GROUP]).astype(BF16)
    x1 = (x_ref[...]
          + jnp.dot(y1, wo_ref[0:GROUP, :], preferred_element_type=F32)
          + jnp.dot(y2, wo_ref[GROUP:2 * GROUP, :], preferred_element_type=F32))
    ms = jnp.mean(x1 * x1, axis=-1, keepdims=True)
    hn = (x1 * lax.rsqrt(ms + NORM_EPS) * pg_ref[...]).astype(BF16)
    z = jnp.dot(hn, wg_ref[...], preferred_element_type=F32)
    gate = 1.0 / (1.0 + jnp.exp(-z))
    pp = jnp.dot(p_ref[...].astype(BF16), wp_ref[...], preferred_element_type=F32)
    out_ref[...] = x1 + pp * gate


def _outproj(o1, o2, gate, x2, p2, w_out, ple_gain, w_gate, w_proj, tm):
    t_tokens = x2.shape[0]
    tok = lambda n: pl.BlockSpec((tm, n), lambda i: (i, 0))
    full = lambda a: pl.BlockSpec(a.shape, lambda i: (0, 0))
    pg = ple_gain.reshape(1, D_MODEL)
    return pl.pallas_call(
        _outproj_kernel,
        grid=(t_tokens // tm,),
        in_specs=[tok(GROUP), tok(GROUP), tok(2 * GROUP), tok(D_MODEL), tok(PLE_DIM),
                  full(w_out), full(pg), full(w_gate), full(w_proj)],
        out_specs=tok(D_MODEL),
        out_shape=jax.ShapeDtypeStruct((t_tokens, D_MODEL), F32),
        compiler_params=_params("parallel"),
        name="outproj_ple",
    )(o1, o2, gate, x2, p2, w_out, pg, w_gate, w_proj)


TM = 256
TQ_DSA = 128
TK_DSA = 512
TQ_WINDOW = 256
T_DILATED = 512
T_DIFF = 512


def _rope_tables(s_len):
    half = HEAD_DIM // 2
    inv = ROPE_THETA ** (-jnp.arange(half, dtype=F32) / half)
    ang = jnp.arange(s_len).astype(F32)[:, None] * inv[None, :]
    cos = jnp.tile(jnp.cos(ang), (1, 4))
    sin = jnp.sin(ang)
    sin_signed = jnp.tile(jnp.concatenate([-sin, sin], axis=1), (1, 2))
    return cos, sin_signed


def _pair_gain(g):
    return jnp.concatenate([g, g]).astype(F32)


def _even_layer(x2, p2, bsz, s_len, tables, norm_gain, w_in, w_out, a_q_gain, a_k_gain, idx_k_gain,
                b_q_gain, b_k_gain, b_sinks, ple_gain, w_gate, w_proj):
    hd = HEAD_DIM
    offs = np.cumsum([0, 512, hd, hd, 512, hd, 8, 512, 512, 2 * hd, 2 * hd, 512])
    aq, ak, av, iq, ik, iw, ag, bq, bk, bv, bg = [w_in[:, offs[n]:offs[n + 1]] for n in range(11)]
    dup = lambda c: jnp.concatenate([c, c], axis=1)
    w = jnp.concatenate(
        [aq, iq, bq, ag, bg,
         dup(ak), dup(ik), dup(bk[:, :hd]), dup(bk[:, hd:]),
         dup(av), dup(bv[:, :hd]), dup(bv[:, hd:]),
         jnp.pad(iw, ((0, 0), (0, LANES - 8)))], axis=1).astype(BF16)
    ones = jnp.ones((LANES,), F32)
    gains = jnp.stack(
        [_pair_gain(a_q_gain)] * 4 + [ones] * 4 + [_pair_gain(b_q_gain)] * 4 + [ones] * 8
        + [_pair_gain(a_k_gain), _pair_gain(idx_k_gain), _pair_gain(b_k_gain), _pair_gain(b_k_gain)]
        + [ones] * 4)
    idx_scale = (8 * hd) ** -0.5
    plan = ([("k", 0, r, True, QSCALE) for r in range(4)]
            + [("k", 1, r, False, 1.0) for r in range(4)]
            + [("q", 2, c, True, QSCALE) for c in range(4)]
            + [("silu", 3, c) for c in range(8)]
            + [("q", 4, 0, True, 1.0)]
            + [("q", 5, 0, True, 1.0)]
            + [("k", 6, r, True, 1.0) for r in range(2)]
            + [("vt", 7, 0, HEAD_DIM, 1)]
            + [("v2", 8, 2 * r) for r in range(2)]
            + [("iw", 9, idx_scale)])
    out_defs = [("kt", 4, BF16), ("kt", 4, BF16), ("tok", 512, BF16), ("tok", 1024, ACT),
                ("tok", LANES, BF16), ("tok", LANES, BF16), ("kt", 2, BF16),
                ("vt", (1, HEAD_DIM + VT_PAD), BF16), ("v", 4, BF16), ("rows", 8, F32)]
    qaT, qiT, qb, gate, ka, ki, kbT, vaT, vb, iwT = _inproj(x2, norm_gain, w, gains, *tables, plan,
                                                            out_defs, TM)
    oa = _dsa_attention(qiT, iwT, ki, qaT, ka, vaT, bsz, s_len, min(TQ_DSA, s_len), TK_DSA)
    ob = _window_attention(qb, kbT, vb, b_sinks.astype(F32), bsz, s_len, TQ_WINDOW)
    return _outproj(oa, ob, gate, x2, p2, w_out.astype(BF16), ple_gain, w_gate.astype(BF16),
                    w_proj.astype(BF16), TM)


def _odd_layer(x2, p2, bsz, s_len, tables, norm_gain, w_in, w_out, c_q_gain, c_k_gain, d_q_gain,
               d_k_gain, lq1, lk1, lq2, lk2, sub_gain, lambda_init, ple_gain, w_gate, w_proj):
    ones = jnp.ones((LANES,), F32)
    gains = jnp.stack([_pair_gain(c_q_gain)] * 4 + [_pair_gain(c_k_gain)] * 4 + [ones] * 8
                      + [_pair_gain(d_q_gain)] * 4 + [_pair_gain(d_k_gain)] * 4 + [ones] * 8)
    plan = ([("k", 0, r, True, QSCALE) for r in range(4)]
            + [("q", 1, c, True, 1.0) for c in range(4)]
            + [("vt", 2, r, HEAD_DIM, 2) for r in range(4)]
            + [("silu", 3, c) for c in range(4)]
            + [("k", 4, r, True, QSCALE) for r in range(4)]
            + [("q", 5, c, True, 1.0) for c in range(4)]
            + [("vt", 6, r, 2 * HEAD_DIM, 1) for r in range(4)]
            + [("silu", 3, 4 + c) for c in range(4)])
    out_defs = [("kt", 4, BF16), ("tok", 512, BF16), ("vt", (4, 2 * (HEAD_DIM + VT_PAD)), BF16),
                ("tok", 1024, ACT),
                ("kt", 4, BF16), ("tok", 512, BF16), ("vt", (4, 2 * HEAD_DIM + VT_PAD), BF16)]
    qcT, kc, vcT, gate, qdT, kd, vdT = _inproj(x2, norm_gain, w_in.astype(BF16), gains, *tables,
                                               plan, out_defs, TM)
    oc = _dilated_attention(qcT, kc, vcT, bsz, s_len, T_DILATED)
    od = _diff_attention(qdT, kd, vdT, lq1, lk1, lq2, lk2, sub_gain, lambda_init, bsz, s_len, T_DIFF)
    return _outproj(oc, od, gate, x2, p2, w_out.astype(BF16), ple_gain, w_gate.astype(BF16),
                    w_proj.astype(BF16), TM)


def kernel(x, p, norm_gain, w_in_even, w_out_even, a_q_gain, a_k_gain, idx_k_gain, b_q_gain, b_k_gain, b_sinks, w_in_odd, w_out_odd, c_q_gain, c_k_gain, d_q_gain, d_k_gain, d_lambda_q1, d_lambda_k1, d_lambda_q2, d_lambda_k2, d_subln_gain, ple_norm_gain, w_ple_gate, w_ple_proj):
    bsz, s_len, d_model = x.shape
    depth = p.shape[0]
    assert d_model == D_MODEL and s_len % 512 == 0
    tables = _rope_tables(s_len)
    x2 = x.reshape(bsz * s_len, d_model)
    for i in range(depth):
        j = i // 2
        p2 = p[i].reshape(bsz * s_len, PLE_DIM)
        if i % 2 == 0:
            x2 = _even_layer(x2, p2, bsz, s_len, tables, norm_gain[i], w_in_even[j], w_out_even[j],
                             a_q_gain[j], a_k_gain[j], idx_k_gain[j], b_q_gain[j], b_k_gain[j],
                             b_sinks[j], ple_norm_gain[i], w_ple_gate[i], w_ple_proj[i])
        else:
            lambda_init = 0.8 - 0.6 * math.exp(-0.3 * i)
            x2 = _odd_layer(x2, p2, bsz, s_len, tables, norm_gain[i], w_in_odd[j], w_out_odd[j],
                            c_q_gain[j], c_k_gain[j], d_q_gain[j], d_k_gain[j], d_lambda_q1[j],
                            d_lambda_k1[j], d_lambda_q2[j], d_lambda_k2[j], d_subln_gain[j],
                            lambda_init, ple_norm_gain[i], w_ple_gate[i], w_ple_proj[i])
    return x2.reshape(bsz, s_len, d_model)
```
